```python
import jax, jax.numpy as jnp
from jax import lax
import numpy as np

D_MODEL = 2048
BATCH = 4
SEQ = 4096
DEPTH = 2

RWKV_HEADS = 16
RWKV_HEAD_DIM = 64
RWKV_WIDTH = RWKV_HEADS * RWKV_HEAD_DIM
DECAY_LORA = 96
ICLR_LORA = 96
VRES_LORA = 64
GN_EPS = 64e-5
L2_EPS = 1e-12
SHIFT_WIDTH = 3 * RWKV_WIDTH + DECAY_LORA + ICLR_LORA

NSA_HEADS = 16
NSA_KV_GROUPS = 4
NSA_HEAD_DIM = 64
NSA_WIDTH = NSA_HEADS * NSA_HEAD_DIM
NSA_KV_WIDTH = NSA_KV_GROUPS * NSA_HEAD_DIM
CMP_LEN = 32
CMP_STRIDE = 16
CMP_HIDDEN = 256
SEL_LEN = 64
SEL_TOPK = 16
N_LOCAL_SEL = 2
WINDOW = 512
Q_BLOCK = 64
ROPE_THETA = 500000.0
ROT_DIM = NSA_HEAD_DIM // 4
FORCE_BONUS = 1e3
MASK_VALUE = -1e30
MAX_POS_OFFSET = 1024

NORM_EPS = 1e-6
IN_SPLITS = (SHIFT_WIDTH, RWKV_WIDTH, NSA_WIDTH, 6 * NSA_KV_WIDTH, 3 * NSA_HEADS, NSA_WIDTH, 2 * D_MODEL)
IN_WIDTH = SHIFT_WIDTH + RWKV_WIDTH + NSA_WIDTH + 6 * NSA_KV_WIDTH + 3 * NSA_HEADS + NSA_WIDTH + 2 * D_MODEL

kernel_name = 'hybrid_rwkv7_nsa_block'


def rms_norm(x, gain):
    xf = x.astype(jnp.float32)
    y = xf * lax.rsqrt(jnp.mean(xf * xf, axis=-1, keepdims=True) + NORM_EPS)
    return y * gain.astype(jnp.float32)


def partial_rotary(t, pos):
    half = ROT_DIM // 2
    inv_freq = ROPE_THETA ** (-jnp.arange(half, dtype=jnp.float32) * (2.0 / ROT_DIM))
    ang = pos.astype(jnp.float32)[..., None] * inv_freq
    ang = ang.reshape(pos.shape + (1,) * (t.ndim - pos.ndim - 1) + (half,))
    cos, sin = jnp.cos(ang), jnp.sin(ang)
    t1, t2 = t[..., :half], t[..., half:ROT_DIM]
    return jnp.concatenate([t1 * cos - t2 * sin, t1 * sin + t2 * cos, t[..., ROT_DIM:]], axis=-1)


def wkv7_scan(r, decay, k, v, kk, a):
    B, S, H, N = r.shape

    def step(state, inp):
        r_t, w_t, k_t, v_t, kk_t, a_t = inp
        s_kk = jnp.einsum('bhvk,bhk->bhv', state, kk_t)
        state = (state * w_t[:, :, None, :]
                 - s_kk[..., None] * (kk_t * a_t)[:, :, None, :]
                 + v_t[..., None] * k_t[:, :, None, :])
        return state, jnp.einsum('bhvk,bhk->bhv', state, r_t)

    xs = tuple(jnp.moveaxis(t, 1, 0) for t in (r, decay, k, v, kk, a))
    state0 = jnp.zeros((B, H, N, N), jnp.float32)
    _, y = lax.scan(step, state0, xs)
    return jnp.moveaxis(y, 0, 1)


def rwkv7_time_mix(u, mu, w0, w_up, a0, a_up, k_k, k_a, r_k, gn_w, gn_b, v_first, vres):
    f32 = jnp.float32
    u = u.astype(f32)
    B, S, _ = u.shape
    H, N, C = RWKV_HEADS, RWKV_HEAD_DIM, RWKV_WIDTH
    prev = jnp.pad(u[:, :-1], ((0, 0), (1, 0), (0, 0)))
    m = u + (prev - u) * mu
    r, k, v = m[..., :C], m[..., C:2 * C], m[..., 2 * C:3 * C]
    w_lo = m[..., 3 * C:3 * C + DECAY_LORA]
    a_lo = m[..., 3 * C + DECAY_LORA:]
    w = -jax.nn.softplus(-(w0 + jnp.tanh(w_lo) @ w_up)) - 0.5
    decay = jnp.exp(-jnp.exp(w))
    a = jax.nn.sigmoid(a0 + a_lo @ a_up)
    if vres is None:
        v_first = v
    else:
        v0, v_dn, v_upm = vres
        v = v + (v_first - v) * jax.nn.sigmoid(v0 + (v @ v_dn) @ v_upm)
    heads = lambda t: t.reshape(B, S, H, N)
    kk = heads(k * k_k)
    kk = kk / jnp.maximum(jnp.sqrt(jnp.sum(kk * kk, axis=-1, keepdims=True)), L2_EPS)
    k = k * (1.0 + (a - 1.0) * k_a)
    r, k, v, decay, a = heads(r), heads(k), heads(v), heads(decay), heads(a)
    y = wkv7_scan(r, decay, k, v, kk, a)
    mean = jnp.mean(y, axis=-1, keepdims=True)
    var = jnp.mean(jnp.square(y - mean), axis=-1, keepdims=True)
    y = ((y - mean) * lax.rsqrt(var + GN_EPS)).reshape(B, S, C) * gn_w + gn_b
    bonus = (jnp.sum(r * k * r_k, axis=-1, keepdims=True) * v).reshape(B, S, C)
    return y + bonus, v_first


def compress_blocks(t, pe, w1, w2):
    B, S, G, dh = t.shape
    n_cmp = (S - CMP_LEN) // CMP_STRIDE + 1
    c = t.reshape(B, S // CMP_STRIDE, CMP_STRIDE, G, dh)
    blocks = jnp.concatenate([c[:, m:m + n_cmp] for m in range(CMP_LEN // CMP_STRIDE)], axis=2)
    blocks = blocks + pe[None, None, :, None, :]
    flat = blocks.transpose(0, 1, 3, 2, 4).reshape(B, n_cmp, G, CMP_LEN * dh)
    return jax.nn.gelu(flat @ w1) @ w2


def nsa_attention(q, kv, gate_logits, positions, pe_k, pe_v, ck_w1, ck_w2, cv_w1, cv_w2):
    f32 = jnp.float32
    B, S, _ = q.shape
    G, HG, DH = NSA_KV_GROUPS, NSA_HEADS // NSA_KV_GROUPS, NSA_HEAD_DIM
    scale = DH ** -0.5
    n_cmp = (S - CMP_LEN) // CMP_STRIDE + 1
    n_sel = S // SEL_LEN
    sel_k = min(SEL_TOPK, n_sel)
    q = partial_rotary(q.astype(f32).reshape(B, S, G, HG, DH), positions)
    k_cmp, v_cmp, k_slc, v_slc, k_win, v_win = [t.astype(f32).reshape(B, S, G, DH) for t in jnp.split(kv, 6, axis=-1)]
    cmp_end = jnp.arange(n_cmp) * CMP_STRIDE + CMP_LEN - 1
    kc = partial_rotary(compress_blocks(k_cmp, pe_k, ck_w1, ck_w2), positions[:, cmp_end])
    vc = compress_blocks(v_cmp, pe_v, cv_w1, cv_w2)
    ksel = partial_rotary(k_slc, positions).reshape(B, n_sel, SEL_LEN, G, DH).transpose(0, 3, 1, 2, 4)
    vsel = v_slc.reshape(B, n_sel, SEL_LEN, G, DH).transpose(0, 3, 1, 2, 4)
    kwin = jnp.pad(partial_rotary(k_win, positions), ((0, 0), (WINDOW, 0), (0, 0), (0, 0)))
    vwin = jnp.pad(v_win, ((0, 0), (WINDOW, 0), (0, 0), (0, 0)))
    ci = jnp.arange(n_cmp)[:, None] * CMP_STRIDE
    sj = jnp.arange(n_sel)[None, :] * SEL_LEN
    overlap = ((ci < sj + SEL_LEN) & (ci + CMP_LEN > sj)).astype(f32)
    blk = jnp.arange(n_sel)
    bi = jnp.arange(B)[:, None, None, None]
    gi = jnp.arange(G)[None, :, None, None]

    def block_fn(c):
        t0 = c * Q_BLOCK
        qb = lax.dynamic_slice_in_dim(q, t0, Q_BLOCK, axis=1)
        tq = t0 + jnp.arange(Q_BLOCK)
        cmp_ok = cmp_end[None, :] <= tq[:, None]
        s_cmp = jnp.einsum('bqghd,bngd->bghqn', qb, kc) * scale
        p_cmp = jax.nn.softmax(jnp.where(cmp_ok, s_cmp, MASK_VALUE), axis=-1) * cmp_ok
        o_cmp = jnp.einsum('bghqn,bngd->bqghd', p_cmp, vc)
        imp = jnp.einsum('bghqn,nj->bgqj', p_cmp, overlap)
        cur = tq // SEL_LEN
        dist = cur[:, None] - blk[None, :]
        forced = (blk[None, :] == 0) | ((dist >= 0) & (dist < N_LOCAL_SEL))
        causal_blk = blk[None, :] * SEL_LEN <= tq[:, None]
        imp = jnp.where(causal_blk, imp + jnp.where(forced, FORCE_BONUS, 0.0), MASK_VALUE)
        _, idx = lax.top_k(imp, sel_k)
        ks = ksel[bi, gi, idx]
        vs = vsel[bi, gi, idx]
        kpos = idx[..., None] * SEL_LEN + jnp.arange(SEL_LEN)
        sel_ok = kpos <= tq[None, None, :, None, None]
        s_slc = jnp.einsum('bqghd,bgqksd->bghqks', qb, ks) * scale
        s_slc = jnp.where(sel_ok[:, :, None], s_slc, MASK_VALUE)
        p_slc = jax.nn.softmax(s_slc.reshape(B, G, HG, Q_BLOCK, sel_k * SEL_LEN), axis=-1)
        o_slc = jnp.einsum('bghqks,bgqksd->bqghd', p_slc.reshape(s_slc.shape), vs)
        kw = lax.dynamic_slice_in_dim(kwin, t0, Q_BLOCK + WINDOW, axis=1)
        vw = lax.dynamic_slice_in_dim(vwin, t0, Q_BLOCK + WINDOW, axis=1)
        kwpos = t0 - WINDOW + jnp.arange(Q_BLOCK + WINDOW)
        diff = tq[:, None] - kwpos[None, :]
        win_ok = (diff >= 0) & (diff < WINDOW) & (kwpos[None, :] >= 0)
        s_win = jnp.einsum('bqghd,bkgd->bghqk', qb, kw) * scale
        p_win = jax.nn.softmax(jnp.where(win_ok, s_win, MASK_VALUE), axis=-1)
        o_win = jnp.einsum('bghqk,bkgd->bqghd', p_win, vw)
        return o_cmp, o_slc, o_win

    o_cmp, o_slc, o_win = lax.map(block_fn, jnp.arange(S // Q_BLOCK))
    unblock = lambda o: jnp.moveaxis(o, 0, 1).reshape(B, S, NSA_HEADS, DH)
    g = jax.nn.sigmoid(gate_logits.astype(f32)).reshape(B, S, 3, NSA_HEADS, 1)
    o = g[:, :, 0] * unblock(o_cmp) + g[:, :, 1] * unblock(o_slc) + g[:, :, 2] * unblock(o_win)
    return o.reshape(B, S, NSA_WIDTH)


def setup_inputs(seed: int = 0) -> dict:
    key = jax.random.key(seed)
    ks = jax.random.split(key, 32)
    L = DEPTH
    f32 = jnp.float32

    def nrm(k, shape, scale):
        return scale * jax.random.normal(k, shape, f32)

    x = jax.random.normal(ks[0], (BATCH, SEQ, D_MODEL), f32)
    positions = (jnp.arange(SEQ, dtype=jnp.int32)[None, :]
                 + jax.random.randint(ks[1], (BATCH, 1), 0, MAX_POS_OFFSET, dtype=jnp.int32))
    return {
        'x': x,
        'positions': positions,
        'norm_pre': 1.0 + nrm(ks[2], (L, D_MODEL), 0.05),
        'norm_post': 1.0 + nrm(ks[3], (L, D_MODEL), 0.05),
        'w_in': nrm(ks[4], (L, D_MODEL, IN_WIDTH), D_MODEL ** -0.5),
        'rwkv_mu': jax.random.uniform(ks[5], (L, SHIFT_WIDTH), f32),
        'rwkv_w0': jax.random.uniform(ks[6], (L, RWKV_WIDTH), f32, -6.5, -1.5),
        'rwkv_w_up': nrm(ks[7], (L, DECAY_LORA, RWKV_WIDTH), 0.1 * DECAY_LORA ** -0.5),
        'rwkv_a0': nrm(ks[8], (L, RWKV_WIDTH), 0.1),
        'rwkv_a_up': nrm(ks[9], (L, ICLR_LORA, RWKV_WIDTH), ICLR_LORA ** -0.5),
        'rwkv_k_k': 0.85 + nrm(ks[10], (L, RWKV_WIDTH), 0.05),
        'rwkv_k_a': 1.0 + nrm(ks[11], (L, RWKV_WIDTH), 0.05),
        'rwkv_r_k': nrm(ks[12], (L, RWKV_HEADS, RWKV_HEAD_DIM), 0.1),
        'rwkv_gn_w': 1.0 + nrm(ks[13], (L, RWKV_WIDTH), 0.05),
        'rwkv_gn_b': nrm(ks[14], (L, RWKV_WIDTH), 0.02),
        'rwkv_v0': 1.0 + nrm(ks[15], (L - 1, RWKV_WIDTH), 0.1),
        'rwkv_v_down': nrm(ks[16], (L - 1, RWKV_WIDTH, VRES_LORA), RWKV_WIDTH ** -0.5),
        'rwkv_v_up': nrm(ks[17], (L - 1, VRES_LORA, RWKV_WIDTH), VRES_LORA ** -0.5),
        'nsa_pe_k': nrm(ks[18], (L, CMP_LEN, NSA_HEAD_DIM), 0.1),
        'nsa_pe_v': nrm(ks[19], (L, CMP_LEN, NSA_HEAD_DIM), 0.1),
        'nsa_ck_w1': nrm(ks[20], (L, CMP_LEN * NSA_HEAD_DIM, CMP_HIDDEN), (CMP_LEN * NSA_HEAD_DIM) ** -0.5),
        'nsa_ck_w2': nrm(ks[21], (L, CMP_HIDDEN, NSA_HEAD_DIM), CMP_HIDDEN ** -0.5),
        'nsa_cv_w1': nrm(ks[22], (L, CMP_LEN * NSA_HEAD_DIM, CMP_HIDDEN), (CMP_LEN * NSA_HEAD_DIM) ** -0.5),
        'nsa_cv_w2': nrm(ks[23], (L, CMP_HIDDEN, NSA_HEAD_DIM), CMP_HIDDEN ** -0.5),
        'w_proj_a': nrm(ks[24], (L, RWKV_WIDTH, D_MODEL), RWKV_WIDTH ** -0.5),
        'w_proj_b': nrm(ks[25], (L, NSA_WIDTH, D_MODEL), NSA_WIDTH ** -0.5),
        'w_out': nrm(ks[26], (L, D_MODEL, D_MODEL), D_MODEL ** -0.5),
    }


def reference(x, positions, norm_pre, norm_post, w_in, rwkv_mu, rwkv_w0, rwkv_w_up, rwkv_a0, rwkv_a_up,
              rwkv_k_k, rwkv_k_a, rwkv_r_k, rwkv_gn_w, rwkv_gn_b, rwkv_v0, rwkv_v_down, rwkv_v_up,
              nsa_pe_k, nsa_pe_v, nsa_ck_w1, nsa_ck_w2, nsa_cv_w1, nsa_cv_w2, w_proj_a, w_proj_b, w_out):
    f32 = jnp.float32
    split_points = np.cumsum(IN_SPLITS)[:-1].tolist()
    v_first = None
    for l in range(DEPTH):
        h = rms_norm(x, norm_pre[l]).astype(x.dtype)
        proj = h @ w_in[l]
        u_shift, z_a, q, kv, nsa_gate, z_b, merge_gate = jnp.split(proj, split_points, axis=-1)
        vres = None if l == 0 else (rwkv_v0[l - 1], rwkv_v_down[l - 1], rwkv_v_up[l - 1])
        y_a, v_first = rwkv7_time_mix(u_shift, rwkv_mu[l], rwkv_w0[l], rwkv_w_up[l], rwkv_a0[l], rwkv_a_up[l],
                                      rwkv_k_k[l], rwkv_k_a[l], rwkv_r_k[l], rwkv_gn_w[l], rwkv_gn_b[l],
                                      v_first, vres)
        y_b = nsa_attention(q, kv, nsa_gate, positions, nsa_pe_k[l], nsa_pe_v[l],
                            nsa_ck_w1[l], nsa_ck_w2[l], nsa_cv_w1[l], nsa_cv_w2[l])
        y_a = y_a * jax.nn.silu(z_a.astype(f32))
        y_b = y_b * jax.nn.silu(z_b.astype(f32))
        g_a, g_b = jnp.split(jax.nn.sigmoid(merge_gate.astype(f32)), 2, axis=-1)
        merged = g_a * (y_a @ w_proj_a[l]) + g_b * (y_b @ w_proj_b[l])
        out = merged @ w_out[l]
        x = x + rms_norm(out, norm_post[l]).astype(x.dtype)
    return x
```

```python
import functools
import math

import numpy as np
import jax
import jax.numpy as jnp
from jax import lax
from jax.experimental import pallas as pl
from jax.experimental.pallas import tpu as pltpu

F32 = jnp.float32
BF16 = jnp.bfloat16

D_MODEL = 2048
RWKV_HEADS = 16
HEAD_DIM = 64
RWKV_WIDTH = RWKV_HEADS * HEAD_DIM
DECAY_LORA = 96
ICLR_LORA = 96
VRES_LORA = 64
GN_EPS = 64e-5
L2_EPS = 1e-12
SHIFT_WIDTH = 3 * RWKV_WIDTH + DECAY_LORA + ICLR_LORA

NSA_HEADS = 16
NSA_GROUPS = 4
HEADS_PER_GROUP = NSA_HEADS // NSA_GROUPS
NSA_WIDTH = NSA_HEADS * HEAD_DIM
NSA_KV_WIDTH = NSA_GROUPS * HEAD_DIM
CMP_LEN = 32
CMP_STRIDE = 16
CMP_HIDDEN = 256
SEL_LEN = 64
SEL_TOPK = 16
N_LOCAL_SEL = 2
WINDOW = 512
ROPE_THETA = 500000.0
ROT_DIM = HEAD_DIM // 4
FORCE_BONUS = 1e3
MASK_VALUE = -1e30
NORM_EPS = 1e-6

LANES = 128
MXU_DIM = 256
VMEM_LIMIT = 56 * 1024 * 1024

COL_MERGE = 0
COL_R = 4096
COL_K = 5120
COL_V = 6144
COL_ZA = 7168
COL_Q = 8192
COL_KV = 9216
COL_ZB = 10752
COL_MISC = 11776
MISC_WIDTH = 256
GATE_OFF = DECAY_LORA + ICLR_LORA
PROJ_WIDTH = 12288

WKV_CHUNK = 64
HEAD_GROUP_LANES = 256
Q_TILE = 128
KEY_TILE = 512


def _dot(a, b):
    return jnp.dot(a, b, preferred_element_type=F32)


def _dot_nt(a, b):
    return lax.dot_general(a, b, (((1,), (1,)), ((), ())), preferred_element_type=F32)


def _dot_tn(a, b):
    return lax.dot_general(a, b, (((0,), (0,)), ((), ())), preferred_element_type=F32)


def _split2(x):
    hi = x.astype(BF16)
    lo = (x - hi.astype(F32)).astype(BF16)
    return hi, lo


def _dot2(x, w):
    hi, lo = _split2(x)
    return _dot(hi, w) + _dot(lo, w)


def _head_sum(x, ones_bd):
    parts = []
    for c in range(x.shape[1] // LANES):
        parts.append(_dot2(x[:, c * LANES:(c + 1) * LANES], ones_bd))
    return jnp.concatenate(parts, axis=1)


def _sigmoid(x):
    return 1.0 / (1.0 + jnp.exp(-x))


def _head_of(idx):
    return jnp.right_shift(idx, int(math.log2(HEAD_DIM)))


def _params(sem):
    return pltpu.CompilerParams(dimension_semantics=sem, vmem_limit_bytes=VMEM_LIMIT)


def _inproj_kernel(x_ref, g_ref, w_ref, o_ref, h_ref):
    @pl.when(pl.program_id(1) == 0)
    def _():
        x = x_ref[...]
        ms = jnp.mean(x * x, axis=-1, keepdims=True)
        h_ref[...] = (x * lax.rsqrt(ms + NORM_EPS) * g_ref[...]).astype(BF16)

    o_ref[...] = _dot(h_ref[...], w_ref[...])


def _inproj(x, gain, w, tm=1024, tn=512):
    t = x.shape[0]
    return pl.pallas_call(
        _inproj_kernel,
        grid=(t // tm, PROJ_WIDTH // tn),
        in_specs=[
            pl.BlockSpec((tm, D_MODEL), lambda i, j: (i, 0)),
            pl.BlockSpec((1, D_MODEL), lambda i, j: (0, 0)),
            pl.BlockSpec((D_MODEL, tn), lambda i, j: (0, j)),
        ],
        out_specs=pl.BlockSpec((tm, tn), lambda i, j: (i, j)),
        out_shape=jax.ShapeDtypeStruct((t, PROJ_WIDTH), F32),
        scratch_shapes=[pltpu.VMEM((tm, D_MODEL), BF16)],
        compiler_params=_params(("parallel", "arbitrary")),
    )(x, gain, w)


def _rwkv_prep_kernel(has_vres, seq_tiles, *refs):
    (r_ref, k_ref, v_ref, m_ref, rp_ref, kp_ref, vp_ref, mp_ref,
     mu_r, mu_k, mu_v, mu_m, w0_ref, a0_ref, kk_w, ka_w, rk_w, wup_ref, aup_ref, ones_ref) = refs[:20]
    pos = 20
    if has_vres:
        v0_ref, vdn_ref, vup_ref, vf_ref = refs[pos:pos + 4]
        pos += 4
    r_o, lw_o, k_o, v_o, kk_o, b_o, bonus_o = refs[pos:pos + 7]

    first = (pl.program_id(0) % seq_tiles) == 0
    tm = r_ref.shape[0]
    row0 = lax.broadcasted_iota(jnp.int32, (tm, 1), 0) == 0

    def shifted(u_ref, p_ref, mu_ref):
        u = u_ref[...]
        last = p_ref[7:8, :]
        last = jnp.where(first, jnp.zeros_like(last), last)
        prev = jnp.where(row0, last, pltpu.roll(u, 1, axis=0))
        return u + (prev - u) * mu_ref[...]

    r = shifted(r_ref, rp_ref, mu_r)
    k = shifted(k_ref, kp_ref, mu_k)
    v = shifted(v_ref, vp_ref, mu_v)
    misc = shifted(m_ref, mp_ref, mu_m)

    wpre = w0_ref[...] + _dot(jnp.tanh(misc).astype(BF16), wup_ref[...])
    z = -wpre
    softplus = jnp.maximum(z, 0.0) + jnp.log(1.0 + jnp.exp(-jnp.abs(z)))
    w = -softplus - 0.5
    log_decay = -jnp.exp(w)
    a = _sigmoid(a0_ref[...] + _dot(misc.astype(BF16), aup_ref[...]))
    if has_vres:
        low = _dot(v.astype(BF16), vdn_ref[...])
        gate = _sigmoid(v0_ref[...] + _dot(low.astype(BF16), vup_ref[...]))
        v = v + (vf_ref[...] - v) * gate
    ones_bd = ones_ref[...]
    kk = k * kk_w[...]
    norm = jnp.sqrt(_head_sum(kk * kk, ones_bd))
    kk = kk / jnp.maximum(norm, L2_EPS)
    k = k * (1.0 + (a - 1.0) * ka_w[...])
    bonus = _head_sum(r * k * rk_w[...], ones_bd) * v

    r_o[...] = r
    lw_o[...] = log_decay
    k_o[...] = k
    v_o[...] = v
    kk_o[...] = kk
    b_o[...] = kk * a
    bonus_o[...] = bonus


def _rwkv_prep(proj, seq, p, vres, v_first, tm=256):
    t = proj.shape[0]
    c = RWKV_WIDTH
    has_vres = vres is not None

    def cur(width, col):
        return pl.BlockSpec((tm, width), lambda i, _c=col // width: (i, _c))

    def prev(width, col):
        return pl.BlockSpec((8, width),
                            lambda i, _c=col // width: (jnp.maximum(i * (tm // 8) - 1, 0), _c))

    def vec(width):
        return pl.BlockSpec((1, width), lambda i: (0, 0))

    def full(shape):
        return pl.BlockSpec(shape, lambda i: (0,) * len(shape))

    in_specs = [cur(c, COL_R), cur(c, COL_K), cur(c, COL_V), cur(MISC_WIDTH, COL_MISC),
                prev(c, COL_R), prev(c, COL_K), prev(c, COL_V), prev(MISC_WIDTH, COL_MISC),
                vec(c), vec(c), vec(c), vec(MISC_WIDTH), vec(c), vec(c), vec(c), vec(c), vec(c),
                full((MISC_WIDTH, c)), full((MISC_WIDTH, c)), full((LANES, LANES))]
    args = [proj, proj, proj, proj, proj, proj, proj, proj,
            p['mu_r'], p['mu_k'], p['mu_v'], p['mu_m'], p['w0'], p['a0'], p['k_k'], p['k_a'], p['r_k'],
            p['w_up'], p['a_up'], p['ones_bd']]
    if has_vres:
        in_specs += [vec(c), full((c, LANES)), full((LANES, c)),
                     pl.BlockSpec((tm, c), lambda i: (i, 0))]
        args += [vres['v0'], vres['v_dn'], vres['v_up'], v_first]
    out_spec = pl.BlockSpec((tm, c), lambda i: (i, 0))
    out_sds = jax.ShapeDtypeStruct((t, c), F32)
    return pl.pallas_call(
        functools.partial(_rwkv_prep_kernel, has_vres, seq // tm),
        grid=(t // tm,),
        in_specs=in_specs,
        out_specs=[out_spec] * 7,
        out_shape=[out_sds] * 7,
        compiler_params=_params(("parallel",)),
    )(*args)


def _wkv_kernel(r_ref, lw_ref, k_ref, v_ref, kk_ref, b_ref, bonus_ref, za_ref, gnw_ref, gnb_ref,
                ones_ref, o_ref, st_ref):
    L = WKV_CHUNK
    W = HEAD_GROUP_LANES
    n_groups = r_ref.shape[1] // W

    @pl.when(pl.program_id(1) == 0)
    def _():
        st_ref[...] = jnp.zeros_like(st_ref)

    ti = lax.broadcasted_iota(jnp.int32, (L, L), 0)
    tj = lax.broadcasted_iota(jnp.int32, (L, L), 1)
    tri = jnp.where(ti >= tj, 1.0, 0.0).astype(BF16)

    lw = lw_ref[...]
    lw_hi, lw_lo = _split2(lw)
    cum = _dot(tri, lw_hi) + _dot(tri, lw_lo)
    cum_prev = cum - lw
    pend = cum[L - 1:L, :]
    e_n = jnp.exp(-cum)
    e_e = jnp.exp(pend - cum)
    b = b_ref[...]
    k = k_ref[...]
    ab_all = -(jnp.exp(cum_prev) * kk_ref[...])
    rb_all = jnp.exp(cum) * r_ref[...]
    bb_all = e_n * b
    kb_all = e_n * k
    bp_all = e_e * b
    kp_all = e_e * k
    v_all = v_ref[...]
    p_end = jnp.exp(pend)

    ri = lax.broadcasted_iota(jnp.int32, (W, W), 0)
    ci = lax.broadcasted_iota(jnp.int32, (W, W), 1)
    same_head = _head_of(ri) == _head_of(ci)
    incl = same_head & (ri >= ci)
    strict = same_head & (ri > ci)
    eye = jnp.where(ri == ci, 1.0, 0.0)

    def stack4(a):
        return jnp.concatenate([a] * 4, axis=0)

    def xform(a):
        return jnp.where(same_head, stack4(a), 0.0).astype(BF16)

    ys = []
    for g in range(n_groups):
        sl = slice(g * W, (g + 1) * W)
        abx = xform(ab_all[:, sl])
        rbx = xform(rb_all[:, sl])
        vx = xform(v_all[:, sl])
        kpx = xform(kp_all[:, sl])
        bpx = xform(bp_all[:, sl])
        b4 = stack4(bb_all[:, sl]).astype(BF16)
        k4 = stack4(kb_all[:, sl]).astype(BF16)

        nm = jnp.where(strict, _dot_nt(abx, b4), 0.0)
        ak = jnp.where(strict, _dot_nt(abx, k4), 0.0).astype(BF16)
        rbm = jnp.where(incl, _dot_nt(rbx, b4), 0.0).astype(BF16)
        rkm = jnp.where(incl, _dot_nt(rbx, k4), 0.0).astype(BF16)

        tinv = eye + nm
        pw = nm.astype(BF16)
        for _ in range(5):
            pw = _dot(pw, pw).astype(BF16)
            tinv = tinv + _dot(pw, tinv.astype(BF16))
        tb = tinv.astype(BF16)

        uv = _dot(tb, _dot(ak, vx).astype(BF16))
        wa = _dot(tb, abx).astype(BF16)
        rkv = _dot(rkm, vx)
        kv = _dot_tn(kpx, vx)

        st = st_ref[g]
        stb = st.astype(BF16)
        u = _dot(wa, stb) + uv
        ub = u.astype(BF16)
        y = _dot(rbx, stb) + _dot(rbm, ub) + rkv
        pcol = jnp.transpose(jnp.broadcast_to(p_end[:, sl], (W, W)))
        st_ref[g] = pcol * st + _dot_tn(bpx, ub) + kv
        ys.append(y[0:L] + y[L:2 * L] + y[2 * L:3 * L] + y[3 * L:4 * L])

    y = jnp.concatenate(ys, axis=1)
    ones_bd = ones_ref[...]
    inv_n = 1.0 / HEAD_DIM
    mean = _head_sum(y, ones_bd) * inv_n
    d = y - mean
    var = _head_sum(d * d, ones_bd) * inv_n
    yn = d * lax.rsqrt(var + GN_EPS) * gnw_ref[...] + gnb_ref[...]
    za = za_ref[...]
    o_ref[...] = ((yn + bonus_ref[...]) * (za * _sigmoid(za))).astype(BF16)


def _wkv(prep, proj, seq, p):
    r, lw, k, v, kk, b, bonus = prep
    t, c = r.shape
    batch = t // seq
    n_chunks = seq // WKV_CHUNK
    blk = pl.BlockSpec((WKV_CHUNK, c), lambda bi, ci: (bi * n_chunks + ci, 0))
    za = pl.BlockSpec((WKV_CHUNK, c), lambda bi, ci: (bi * n_chunks + ci, COL_ZA // c))
    vec = pl.BlockSpec((1, c), lambda bi, ci: (0, 0))
    return pl.pallas_call(
        _wkv_kernel,
        grid=(batch, n_chunks),
        in_specs=[blk] * 7 + [za, vec, vec, pl.BlockSpec((LANES, LANES), lambda bi, ci: (0, 0))],
        out_specs=blk,
        out_shape=jax.ShapeDtypeStruct((t, c), BF16),
        scratch_shapes=[pltpu.VMEM((c // HEAD_GROUP_LANES, HEAD_GROUP_LANES, HEAD_GROUP_LANES), F32)],
        compiler_params=_params(("parallel", "arbitrary")),
    )(r, lw, k, v, kk, b, bonus, proj, p['gn_w'], p['gn_b'], p['ones_bd'])


def _rotary(x, cos_t, sin_up, sin_dn):
    reps = x.shape[1] // LANES
    half = ROT_DIM // 2
    tile = lambda tbl: jnp.concatenate([tbl] * reps, axis=1) if reps > 1 else tbl
    return (x * tile(cos_t) + pltpu.roll(x, half, axis=1) * tile(sin_up)
            + pltpu.roll(x, x.shape[1] - half, axis=1) * tile(sin_dn))


def _nsa_prep_kernel(q_ref, kv_ref, cos_ref, sup_ref, sdn_ref, sel_ref,
                     q_o, ks_o, vs_o, kw_o, vw_o):
    cos_t, sin_up, sin_dn = cos_ref[...], sup_ref[...], sdn_ref[...]
    q = _rotary(q_ref[...], cos_t, sin_up, sin_dn)
    q_o[...] = (q * (HEAD_DIM ** -0.5)).astype(BF16)
    w = NSA_KV_WIDTH
    kv = kv_ref[...]
    ks = _rotary(kv[:, 2 * w:3 * w], cos_t, sin_up, sin_dn).astype(BF16)
    vs = kv[:, 3 * w:4 * w].astype(BF16)
    kw = _rotary(kv[:, 4 * w:5 * w], cos_t, sin_up, sin_dn).astype(BF16)
    vw = kv[:, 5 * w:6 * w].astype(BF16)
    for g in range(NSA_GROUPS):
        sel = sel_ref[g]
        ks_o[g] = _dot(ks, sel).astype(BF16)
        vs_o[g] = _dot(vs, sel).astype(BF16)
        kw_o[g] = _dot(kw, sel).astype(BF16)
        vw_o[g] = _dot(vw, sel).astype(BF16)


def _nsa_prep(proj, batch, seq, tables, sel, tr=256):
    t = proj.shape[0]
    n_t = seq // tr
    row = lambda bi, i: bi * n_t + i
    tbl = pl.BlockSpec((tr, LANES), lambda bi, i: (row(bi, i), 0))
    kv_o = pl.BlockSpec((None, NSA_GROUPS, tr, MXU_DIM), lambda bi, i: (bi, 0, i, 0))
    kv_sds = jax.ShapeDtypeStruct((batch, NSA_GROUPS, seq, MXU_DIM), BF16)
    return pl.pallas_call(
        _nsa_prep_kernel,
        grid=(batch, n_t),
        in_specs=[
            pl.BlockSpec((tr, NSA_WIDTH), lambda bi, i: (row(bi, i), COL_Q // NSA_WIDTH)),
            pl.BlockSpec((tr, 6 * NSA_KV_WIDTH), lambda bi, i: (row(bi, i), COL_KV // (6 * NSA_KV_WIDTH))),
            tbl, tbl, tbl,
            pl.BlockSpec((NSA_GROUPS, MXU_DIM, MXU_DIM), lambda bi, i: (0, 0, 0)),
        ],
        out_specs=[pl.BlockSpec((tr, NSA_WIDTH), lambda bi, i: (row(bi, i), 0)), kv_o, kv_o, kv_o, kv_o],
        out_shape=[jax.ShapeDtypeStruct((t, NSA_WIDTH), BF16), kv_sds, kv_sds, kv_sds, kv_sds],
        compiler_params=_params(("parallel", "parallel")),
    )(proj, proj, tables[0], tables[1], tables[2], sel)


def _compress_kernel(n_cmp, kc_ref, vc_ref, pek_ref, pev_ref, kw1_ref, kw2_ref, vw1_ref, vw2_ref,
                     cos_ref, sup_ref, sdn_ref, ko_ref, vo_ref):
    half = kw1_ref.shape[0] // 2
    rows = kc_ref.shape[0]
    valid = lax.broadcasted_iota(jnp.int32, (rows, 1), 0) < n_cmp

    def mlp(c_ref, pe_ref, w1_ref, w2_ref):
        c = c_ref[...].astype(BF16)
        pe = jnp.broadcast_to(pe_ref[...], (8, 2 * half)).astype(BF16)
        h = (_dot(c, w1_ref[0:half, :])
             + pltpu.roll(_dot(c, w1_ref[half:2 * half, :]), rows - 1, axis=0)
             + _dot(pe, w1_ref[...])[0:1, :])
        inner = math.sqrt(2.0 / math.pi) * (h + 0.044715 * (h * h * h))
        h = 0.5 * h * (1.0 + jnp.tanh(inner))
        return _dot(h.astype(BF16), w2_ref[...])

    kc = _rotary(mlp(kc_ref, pek_ref, kw1_ref, kw2_ref), cos_ref[...], sup_ref[...], sdn_ref[...])
    ko_ref[...] = jnp.where(valid, kc, 0.0).astype(BF16)
    vo_ref[...] = jnp.where(valid, mlp(vc_ref, pev_ref, vw1_ref, vw2_ref), 0.0).astype(BF16)


def _compress(kc_in, vc_in, p, cmp_tables, n_cmp):
    bg, rows, width = kc_in.shape
    groups = NSA_GROUPS
    cin = pl.BlockSpec((None, rows, width), lambda i: (i, 0, 0))
    full = lambda shape: pl.BlockSpec(shape, lambda i: (0,) * len(shape))
    tbl = pl.BlockSpec((None, rows, LANES), lambda i: (i // groups, 0, 0))
    out = pl.BlockSpec((None, rows, MXU_DIM), lambda i: (i, 0, 0))
    sds = jax.ShapeDtypeStruct((bg, rows, MXU_DIM), BF16)
    return pl.pallas_call(
        functools.partial(_compress_kernel, n_cmp),
        grid=(bg,),
        in_specs=[cin, cin, full((1, 2 * width)), full((1, 2 * width)),
                  full((2 * width, CMP_HIDDEN)), full((CMP_HIDDEN, MXU_DIM)),
                  full((2 * width, CMP_HIDDEN)), full((CMP_HIDDEN, MXU_DIM)),
                  tbl, tbl, tbl],
        out_specs=[out, out],
        out_shape=[sds, sds],
        compiler_params=_params(("parallel",)),
    )(kc_in, vc_in, p['pe_k'], p['pe_v'], p['ck_w1'], p['ck_w2'], p['cv_w1'], p['cv_w2'], *cmp_tables)


def _nsa_attn_kernel(n_cmp, sel_k, q_ref, kc_ref, vc_ref, ks_ref, vs_ref, kw_ref, vw_ref,
                     gate_ref, zb_ref, eg_ref, ovt_ref, exp_ref, o_ref):
    tq = q_ref.shape[0]
    W = MXU_DIM
    n_blk = ovt_ref.shape[0]
    seq = ks_ref.shape[0]
    t0 = pl.program_id(2) * tq
    H = HEADS_PER_GROUP

    q = q_ref[...]
    lane_head = _head_of(lax.broadcasted_iota(jnp.int32, (tq, W), 1))
    qh = [jnp.where(lane_head == h, q, jnp.zeros_like(q)) for h in range(H)]
    tpos = t0 + lax.broadcasted_iota(jnp.int32, (tq, 1), 0)

    n_pad = kc_ref.shape[0]
    cend = lax.broadcasted_iota(jnp.int32, (tq, n_pad), 1) * CMP_STRIDE + (CMP_LEN - 1)
    col = lax.broadcasted_iota(jnp.int32, (tq, n_pad), 1)
    cmp_ok = (cend <= tpos) & (col < n_cmp)
    kc = kc_ref[...]
    vc = vc_ref[...]
    o_cmp = []
    psum = jnp.zeros((tq, n_pad), F32)
    for h in range(H):
        s = jnp.where(cmp_ok, _dot_nt(qh[h], kc), MASK_VALUE)
        m = jnp.max(s, axis=-1, keepdims=True)
        e = jnp.exp(s - m)
        p = jnp.where(cmp_ok, e / jnp.sum(e, axis=-1, keepdims=True), 0.0)
        psum = psum + p
        o_cmp.append(_dot(p.astype(BF16), vc))

    p_hi, p_lo = _split2(psum)
    ovt = ovt_ref[...]
    imp = _dot_nt(ovt, p_hi) + _dot_nt(ovt, p_lo)
    blk = lax.broadcasted_iota(jnp.int32, (n_blk, tq), 0)
    tlane = t0 + lax.broadcasted_iota(jnp.int32, (n_blk, tq), 1)
    dist = _head_of(tlane) - blk
    forced = (blk == 0) | ((dist >= 0) & (dist < N_LOCAL_SEL))
    causal_blk = blk * SEL_LEN <= tlane
    imp = jnp.where(causal_blk, imp + jnp.where(forced, FORCE_BONUS, 0.0), MASK_VALUE)
    rank = jnp.zeros((n_blk, tq), F32)
    for j in range(n_blk):
        rj = imp[j:j + 1, :]
        ge = jnp.where(rj >= imp, 1.0, 0.0)
        gt = jnp.where(rj > imp, 1.0, 0.0)
        rank = rank + jnp.where(blk > j, ge, gt)
    sel_t = jnp.where(rank < sel_k, 1.0, 0.0)
    if n_blk < LANES:
        sel_t = jnp.concatenate([sel_t, jnp.zeros((LANES - n_blk, tq), F32)], axis=0)
    sel = jnp.transpose(sel_t).astype(BF16)

    tk = KEY_TILE if seq >= KEY_TILE else seq
    n_kt = (t0 + tq + tk - 1) // tk
    kiota = lax.broadcasted_iota(jnp.int32, (tq, tk), 1)

    def body(kt, carry):
        ms, ls, accs = carry
        start = pl.multiple_of(kt * tk, tk)
        blk_mask = _dot(sel, exp_ref[:, pl.ds(start, tk)])
        allowed = (blk_mask > 0.5) & (kiota + start <= tpos)
        kt_ = ks_ref[pl.ds(start, tk), :]
        vt_ = vs_ref[pl.ds(start, tk), :]
        new_m, new_l, new_acc = [], [], []
        for h in range(H):
            s = jnp.where(allowed, _dot_nt(qh[h], kt_), MASK_VALUE)
            m_new = jnp.maximum(ms[h], jnp.max(s, axis=-1, keepdims=True))
            alpha = jnp.exp(ms[h] - m_new)
            e = jnp.where(allowed, jnp.exp(s - m_new), 0.0)
            new_m.append(m_new)
            new_l.append(alpha * ls[h] + jnp.sum(e, axis=-1, keepdims=True))
            new_acc.append(alpha * accs[h] + _dot(e.astype(BF16), vt_))
        return tuple(new_m), tuple(new_l), tuple(new_acc)

    init = (tuple(jnp.full((tq, 1), MASK_VALUE, F32) for _ in range(H)),
            tuple(jnp.zeros((tq, 1), F32) for _ in range(H)),
            tuple(jnp.zeros((tq, W), F32) for _ in range(H)))
    _, ls, accs = lax.fori_loop(0, n_kt, body, init)
    o_slc = [accs[h] / ls[h] for h in range(H)]

    wlen = WINDOW + tq if seq >= WINDOW + tq else seq
    wstart = pl.multiple_of(jnp.maximum(t0 + tq - wlen, 0), tq)
    kpos = wstart + lax.broadcasted_iota(jnp.int32, (tq, wlen), 1)
    diff = tpos - kpos
    win_ok = (diff >= 0) & (diff < WINDOW)
    kw = kw_ref[pl.ds(wstart, wlen), :]
    vw = vw_ref[pl.ds(wstart, wlen), :]
    o_win = []
    for h in range(H):
        s = jnp.where(win_ok, _dot_nt(qh[h], kw), MASK_VALUE)
        m = jnp.max(s, axis=-1, keepdims=True)
        e = jnp.where(win_ok, jnp.exp(s - m), 0.0)
        o_win.append(_dot(e.astype(BF16), vw) / jnp.sum(e, axis=-1, keepdims=True))

    g_hi, g_lo = _split2(_sigmoid(gate_ref[...]))
    gates = [_dot(g_hi, eg_ref[j]) + _dot(g_lo, eg_ref[j]) for j in range(3)]
    out = jnp.zeros((tq, W), F32)
    for h in range(H):
        mix = gates[0] * o_cmp[h] + gates[1] * o_slc[h] + gates[2] * o_win[h]
        out = out + jnp.where(lane_head == h, mix, 0.0)
    zb = zb_ref[...]
    o_ref[...] = (out * (zb * _sigmoid(zb))).astype(BF16)


def _nsa_attn(q_rot, kc4, vc4, kv4, proj, batch, seq, consts, n_cmp, sel_k, tq=Q_TILE):
    t = q_rot.shape[0]
    n_q = seq // tq
    ks4, vs4, kw4, vw4 = kv4
    row = lambda b, g, i: b * n_q + i
    cmp_spec = pl.BlockSpec((None, None) + kc4.shape[1:], lambda b, g, i: (b, g, 0, 0))
    kv_spec = pl.BlockSpec((None, None, seq, MXU_DIM), lambda b, g, i: (b, g, 0, 0))
    return pl.pallas_call(
        functools.partial(_nsa_attn_kernel, n_cmp, sel_k),
        grid=(batch, NSA_GROUPS, n_q),
        in_specs=[
            pl.BlockSpec((tq, MXU_DIM), lambda b, g, i: (row(b, g, i), g)),
            cmp_spec, cmp_spec, kv_spec, kv_spec, kv_spec, kv_spec,
            pl.BlockSpec((tq, LANES), lambda b, g, i: (row(b, g, i), (COL_MISC + LANES) // LANES)),
            pl.BlockSpec((tq, MXU_DIM), lambda b, g, i: (row(b, g, i), COL_ZB // MXU_DIM + g)),
            pl.BlockSpec((None, 3, LANES, MXU_DIM), lambda b, g, i: (g, 0, 0, 0)),
            pl.BlockSpec(consts['overlap_t'].shape, lambda b, g, i: (0, 0)),
            pl.BlockSpec(consts['expand'].shape, lambda b, g, i: (0, 0)),
        ],
        out_specs=pl.BlockSpec((tq, MXU_DIM), lambda b, g, i: (row(b, g, i), g)),
        out_shape=jax.ShapeDtypeStruct((t, NSA_WIDTH), BF16),
        compiler_params=_params(("parallel", "parallel", "arbitrary")),
    )(q_rot, kc4.reshape((batch, NSA_GROUPS) + kc4.shape[1:]), vc4.reshape((batch, NSA_GROUPS) + vc4.shape[1:]),
      ks4, vs4, kw4, vw4, proj, proj, consts['gate_expand'], consts['overlap_t'], consts['expand'])


def _merge_kernel(ya_ref, yb_ref, wa_ref, wb_ref, ga_ref, gb_ref, o_ref):
    pa = _dot(ya_ref[...], wa_ref[...])
    pb = _dot(yb_ref[...], wb_ref[...])
    o_ref[...] = (_sigmoid(ga_ref[...]) * pa + _sigmoid(gb_ref[...]) * pb).astype(BF16)


def _merge(y_a, y_b, w_a, w_b, proj, tm=512):
    t, c = y_a.shape
    yspec = pl.BlockSpec((tm, c), lambda i: (i, 0))
    wspec = pl.BlockSpec((c, D_MODEL), lambda i: (0, 0))
    return pl.pallas_call(
        _merge_kernel,
        grid=(t // tm,),
        in_specs=[yspec, yspec, wspec, wspec,
                  pl.BlockSpec((tm, D_MODEL), lambda i: (i, COL_MERGE // D_MODEL)),
                  pl.BlockSpec((tm, D_MODEL), lambda i: (i, COL_MERGE // D_MODEL + 1))],
        out_specs=pl.BlockSpec((tm, D_MODEL), lambda i: (i, 0)),
        out_shape=jax.ShapeDtypeStruct((t, D_MODEL), BF16),
        compiler_params=_params(("parallel",)),
    )(y_a, y_b, w_a, w_b, proj, proj)


def _outproj_kernel(m_ref, w_ref, g_ref, x_ref, o_ref):
    out = _dot(m_ref[...], w_ref[...])
    ms = jnp.mean(out * out, axis=-1, keepdims=True)
    o_ref[...] = x_ref[...] + out * lax.rsqrt(ms + NORM_EPS) * g_ref[...]


def _outproj(merged, w_out, gain, x, tm=256):
    t = x.shape[0]
    rows = pl.BlockSpec((tm, D_MODEL), lambda i: (i, 0))
    return pl.pallas_call(
        _outproj_kernel,
        grid=(t // tm,),
        in_specs=[rows, pl.BlockSpec((D_MODEL, D_MODEL), lambda i: (0, 0)),
                  pl.BlockSpec((1, D_MODEL), lambda i: (0, 0)), rows],
        out_specs=rows,
        out_shape=jax.ShapeDtypeStruct((t, D_MODEL), F32),
        compiler_params=_params(("parallel",)),
    )(merged, w_out, gain, x)


def _reorder_w_in(w):
    u_end = SHIFT_WIDTH
    za = u_end
    q = za + RWKV_WIDTH
    kv = q + NSA_WIDTH
    gate = kv + 6 * NSA_KV_WIDTH
    zb = gate + 3 * NSA_HEADS
    merge = zb + NSA_WIDTH
    end = merge + 2 * D_MODEL
    d = w.shape[0]
    pieces = [w[:, merge:end], w[:, 0:3 * RWKV_WIDTH], w[:, za:q], w[:, q:kv], w[:, kv:gate], w[:, zb:merge],
              w[:, 3 * RWKV_WIDTH:u_end], w[:, gate:zb],
              jnp.zeros((d, MISC_WIDTH - GATE_OFF - 3 * NSA_HEADS), w.dtype),
              jnp.zeros((d, PROJ_WIDTH - COL_MISC - MISC_WIDTH), w.dtype)]
    return jnp.concatenate(pieces, axis=1).astype(BF16)


def _rope_tables(pos):
    half = ROT_DIM // 2
    inv_freq = ROPE_THETA ** (-jnp.arange(half, dtype=F32) * (2.0 / ROT_DIM))
    ang = pos.astype(F32)[..., None] * inv_freq
    cos, sin = jnp.cos(ang), jnp.sin(ang)
    d = np.arange(LANES) % HEAD_DIM
    idx = d % half
    cos_l, sin_l = cos[..., idx], sin[..., idx]
    cos_t = jnp.where(d < ROT_DIM, cos_l, 1.0)
    sin_up = jnp.where((d >= half) & (d < ROT_DIM), sin_l, 0.0)
    sin_dn = jnp.where(d < half, -sin_l, 0.0)
    return cos_t, sin_up, sin_dn


def _constants(seq):
    n_cmp = (seq - CMP_LEN) // CMP_STRIDE + 1
    n_sel = seq // SEL_LEN
    n_pad = seq // CMP_STRIDE
    ones_bd = (np.arange(LANES)[:, None] // HEAD_DIM == np.arange(LANES)[None, :] // HEAD_DIM)
    src = np.arange(MXU_DIM)[:, None]
    dst = np.arange(MXU_DIM)[None, :]
    sel = np.stack([(src == g * HEAD_DIM + dst % HEAD_DIM) for g in range(NSA_GROUPS)])
    gate_expand = np.zeros((NSA_GROUPS, 3, LANES, MXU_DIM), np.float32)
    gate_lane0 = GATE_OFF - LANES
    for g in range(NSA_GROUPS):
        for j in range(3):
            for h in range(HEADS_PER_GROUP):
                lane = gate_lane0 + j * NSA_HEADS + g * HEADS_PER_GROUP + h
                gate_expand[g, j, lane, h * HEAD_DIM:(h + 1) * HEAD_DIM] = 1.0
    ci = np.arange(n_pad)[:, None] * CMP_STRIDE
    sj = np.arange(n_sel)[None, :] * SEL_LEN
    overlap = (ci < sj + SEL_LEN) & (ci + CMP_LEN > sj) & (np.arange(n_pad)[:, None] < n_cmp)
    expand = (np.arange(LANES)[:, None] == np.arange(seq)[None, :] // SEL_LEN)
    to_bf = lambda a: jnp.asarray(a.astype(np.float32), BF16)
    return dict(ones_bd=to_bf(ones_bd), sel=to_bf(sel), gate_expand=to_bf(gate_expand),
                overlap_t=to_bf(overlap.T), expand=to_bf(expand)), n_cmp, n_sel


def _pad_rows(w, rows, at):
    out = jnp.zeros((rows, w.shape[1]), w.dtype)
    return out.at[at:at + w.shape[0]].set(w)


def kernel(x, positions, norm_pre, norm_post, w_in, rwkv_mu, rwkv_w0, rwkv_w_up, rwkv_a0, rwkv_a_up,
           rwkv_k_k, rwkv_k_a, rwkv_r_k, rwkv_gn_w, rwkv_gn_b, rwkv_v0, rwkv_v_down, rwkv_v_up,
           nsa_pe_k, nsa_pe_v, nsa_ck_w1, nsa_ck_w2, nsa_cv_w1, nsa_cv_w2, w_proj_a, w_proj_b, w_out):
    batch, seq, d_model = x.shape
    depth = w_in.shape[0]
    t = batch * seq
    c = RWKV_WIDTH
    consts, n_cmp, n_sel = _constants(seq)
    sel_k = min(SEL_TOPK, n_sel)
    n_pad = seq // CMP_STRIDE

    tables = _rope_tables(positions.reshape(t))
    cmp_pos = positions[:, CMP_LEN - 1::CMP_STRIDE]
    cmp_pos = jnp.concatenate([cmp_pos, cmp_pos[:, -1:]], axis=1)
    cmp_tables = _rope_tables(cmp_pos)

    row = lambda a: a.reshape(1, -1).astype(F32)
    xf = x.reshape(t, d_model)
    v_first = None
    for l in range(depth):
        w = _reorder_w_in(w_in[l])
        proj = _inproj(xf, row(norm_pre[l]), w)

        mu = rwkv_mu[l]
        rp = dict(
            mu_r=row(mu[0:c]), mu_k=row(mu[c:2 * c]), mu_v=row(mu[2 * c:3 * c]),
            mu_m=row(jnp.concatenate([mu[3 * c:], jnp.zeros((MISC_WIDTH - GATE_OFF,), F32)])),
            w0=row(rwkv_w0[l]), a0=row(rwkv_a0[l]), k_k=row(rwkv_k_k[l]), k_a=row(rwkv_k_a[l]),
            r_k=row(rwkv_r_k[l]),
            w_up=_pad_rows(rwkv_w_up[l], MISC_WIDTH, 0).astype(BF16),
            a_up=_pad_rows(rwkv_a_up[l], MISC_WIDTH, DECAY_LORA).astype(BF16),
            ones_bd=consts['ones_bd'], gn_w=row(rwkv_gn_w[l]), gn_b=row(rwkv_gn_b[l]))
        vres = None
        if l > 0:
            vres = dict(v0=row(rwkv_v0[l - 1]),
                        v_dn=jnp.pad(rwkv_v_down[l - 1], ((0, 0), (0, LANES - VRES_LORA))).astype(BF16),
                        v_up=_pad_rows(rwkv_v_up[l - 1], LANES, 0).astype(BF16))
        prep = _rwkv_prep(proj, seq, rp, vres, v_first)
        if l == 0:
            v_first = prep[3]
        y_a = _wkv(prep, proj, seq, rp)

        q_rot, ks4, vs4, kw4, vw4 = _nsa_prep(proj, batch, seq, tables, consts['sel'])
        kv = proj[:, COL_KV:COL_KV + 2 * NSA_KV_WIDTH].reshape(batch, seq, 2, NSA_GROUPS, HEAD_DIM)
        kv = kv.transpose(2, 0, 3, 1, 4).reshape(2, batch * NSA_GROUPS, n_pad, CMP_STRIDE * HEAD_DIM)
        tile4 = lambda w2: jnp.tile(w2, (1, HEADS_PER_GROUP)).astype(BF16)
        cp = dict(pe_k=nsa_pe_k[l].reshape(1, -1), pe_v=nsa_pe_v[l].reshape(1, -1),
                  ck_w1=nsa_ck_w1[l].astype(BF16), ck_w2=tile4(nsa_ck_w2[l]),
                  cv_w1=nsa_cv_w1[l].astype(BF16), cv_w2=tile4(nsa_cv_w2[l]))
        kc4, vc4 = _compress(kv[0], kv[1], cp, cmp_tables, n_cmp)
        y_b = _nsa_attn(q_rot, kc4, vc4, (ks4, vs4, kw4, vw4), proj, batch, seq, consts, n_cmp, sel_k)

        merged = _merge(y_a, y_b, w_proj_a[l].astype(BF16), w_proj_b[l].astype(BF16), proj)
        xf = _outproj(merged, w_out[l].astype(BF16), row(norm_post[l]), xf)
    return xf.reshape(batch, seq, d_model)
```

```python
import functools
import math

import numpy as np
import jax
import jax.numpy as jnp
from jax import lax
from jax.experimental import pallas as pl
from jax.experimental.pallas import tpu as pltpu

F32 = jnp.float32
BF16 = jnp.bfloat16

D_MODEL = 2048
RWKV_HEADS = 16
HEAD_DIM = 64
RWKV_WIDTH = RWKV_HEADS * HEAD_DIM
DECAY_LORA = 96
ICLR_LORA = 96
VRES_LORA = 64
GN_EPS = 64e-5
L2_EPS = 1e-12
SHIFT_WIDTH = 3 * RWKV_WIDTH + DECAY_LORA + ICLR_LORA

NSA_HEADS = 16
NSA_GROUPS = 4
HEADS_PER_GROUP = NSA_HEADS // NSA_GROUPS
NSA_WIDTH = NSA_HEADS * HEAD_DIM
NSA_KV_WIDTH = NSA_GROUPS * HEAD_DIM
CMP_LEN = 32
CMP_STRIDE = 16
CMP_HIDDEN = 256
SEL_LEN = 64
SEL_TOPK = 16
N_LOCAL_SEL = 2
WINDOW = 512
ROPE_THETA = 500000.0
ROT_DIM = HEAD_DIM // 4
FORCE_BONUS = 1e3
MASK_VALUE = -1e30
NORM_EPS = 1e-6

LANES = 128
MXU_DIM = 256
VMEM_LIMIT = 56 * 1024 * 1024

COL_MERGE = 0
COL_R = 4096
COL_K = 5120
COL_V = 6144
COL_ZA = 7168
COL_Q = 8192
COL_KV = 9216
COL_ZB = 10752
COL_MISC = 11776
MISC_WIDTH = 256
GATE_OFF = DECAY_LORA + ICLR_LORA
PROJ_WIDTH = 12288

WKV_CHUNK = 64
WKV_BATCH = 2
HEAD_GROUP_LANES = 256
Q_TILE = 128
KEY_TILE = 256
V_AUG_ROWS = 16
LOG2E = 1.4426950408889634


def _dot(a, b):
    return jnp.dot(a, b, preferred_element_type=F32)


def _dot_nt(a, b):
    return lax.dot_general(a, b, (((1,), (1,)), ((), ())), preferred_element_type=F32)


def _dot_tn(a, b):
    return lax.dot_general(a, b, (((0,), (0,)), ((), ())), preferred_element_type=F32)


def _split2(x):
    hi = x.astype(BF16)
    lo = (x - hi.astype(F32)).astype(BF16)
    return hi, lo


def _dot2(x, w):
    hi, lo = _split2(x)
    return _dot(hi, w) + _dot(lo, w)


def _head_sum(x, ones_bd):
    parts = []
    for c in range(x.shape[1] // LANES):
        parts.append(_dot2(x[:, c * LANES:(c + 1) * LANES], ones_bd))
    return jnp.concatenate(parts, axis=1)


def _sigmoid(x):
    return 1.0 / (1.0 + jnp.exp(-x))


def _head_of(idx):
    return jnp.right_shift(idx, int(math.log2(HEAD_DIM)))


def _params(sem):
    return pltpu.CompilerParams(dimension_semantics=sem, vmem_limit_bytes=VMEM_LIMIT)


def _inproj_kernel(x_ref, g_ref, w_ref, o_ref, h_ref):
    @pl.when(pl.program_id(1) == 0)
    def _():
        x = x_ref[...]
        ms = jnp.mean(x * x, axis=-1, keepdims=True)
        h_ref[...] = (x * lax.rsqrt(ms + NORM_EPS) * g_ref[...]).astype(BF16)

    o_ref[...] = _dot(h_ref[...], w_ref[...])


def _inproj(x, gain, w, layer, tm=1024, tn=1024):
    t = x.shape[0]
    return pl.pallas_call(
        _inproj_kernel,
        grid=(t // tm, PROJ_WIDTH // tn),
        in_specs=[
            pl.BlockSpec((tm, D_MODEL), lambda i, j: (i, 0)),
            pl.BlockSpec((1, D_MODEL), lambda i, j: (0, 0)),
            pl.BlockSpec((None, D_MODEL, tn), lambda i, j: (layer, 0, j)),
        ],
        out_specs=pl.BlockSpec((tm, tn), lambda i, j: (i, j)),
        out_shape=jax.ShapeDtypeStruct((t, PROJ_WIDTH), F32),
        scratch_shapes=[pltpu.VMEM((tm, D_MODEL), BF16)],
        compiler_params=_params(("parallel", "arbitrary")),
    )(x, gain, w)


def _rwkv_prep_kernel(has_vres, seq_tiles, *refs):
    (r_ref, k_ref, v_ref, m_ref, rp_ref, kp_ref, vp_ref, mp_ref,
     mu_r, mu_k, mu_v, mu_m, w0_ref, a0_ref, kk_w, ka_w, rk_w, wup_ref, aup_ref, ones_ref) = refs[:20]
    pos = 20
    if has_vres:
        v0_ref, vdn_ref, vup_ref, vf_ref = refs[pos:pos + 4]
        pos += 4
    r_o, lw_o, k_o, v_o, kk_o, b_o, bonus_o = refs[pos:pos + 7]

    first = (pl.program_id(0) % seq_tiles) == 0
    tm = r_ref.shape[0]
    row0 = lax.broadcasted_iota(jnp.int32, (tm, 1), 0) == 0

    def shifted(u_ref, p_ref, mu_ref):
        u = u_ref[...]
        last = p_ref[7:8, :]
        last = jnp.where(first, jnp.zeros_like(last), last)
        prev = jnp.where(row0, last, pltpu.roll(u, 1, axis=0))
        return u + (prev - u) * mu_ref[...]

    r = shifted(r_ref, rp_ref, mu_r)
    k = shifted(k_ref, kp_ref, mu_k)
    v = shifted(v_ref, vp_ref, mu_v)
    misc = shifted(m_ref, mp_ref, mu_m)

    wpre = w0_ref[...] + _dot(jnp.tanh(misc).astype(BF16), wup_ref[...])
    z = -wpre
    softplus = jnp.maximum(z, 0.0) + jnp.log(1.0 + jnp.exp(-jnp.abs(z)))
    w = -softplus - 0.5
    log_decay = -jnp.exp(w)
    a = _sigmoid(a0_ref[...] + _dot(misc.astype(BF16), aup_ref[...]))
    if has_vres:
        low = _dot(v.astype(BF16), vdn_ref[...])
        gate = _sigmoid(v0_ref[...] + _dot(low.astype(BF16), vup_ref[...]))
        v = v + (vf_ref[...] - v) * gate
    ones_bd = ones_ref[...]
    kk = k * kk_w[...]
    norm = jnp.sqrt(_head_sum(kk * kk, ones_bd))
    kk = kk / jnp.maximum(norm, L2_EPS)
    k = k * (1.0 + (a - 1.0) * ka_w[...])
    bonus = _head_sum(r * k * rk_w[...], ones_bd) * v

    r_o[...] = r
    lw_o[...] = log_decay
    k_o[...] = k
    v_o[...] = v
    kk_o[...] = kk
    b_o[...] = kk * a
    bonus_o[...] = bonus


def _rwkv_prep(proj, seq, p, vres, v_first, tm=256):
    t = proj.shape[0]
    c = RWKV_WIDTH
    has_vres = vres is not None

    def cur(width, col):
        return pl.BlockSpec((tm, width), lambda i, _c=col // width: (i, _c))

    def prev(width, col):
        return pl.BlockSpec((8, width),
                            lambda i, _c=col // width: (jnp.maximum(i * (tm // 8) - 1, 0), _c))

    def vec(width):
        return pl.BlockSpec((1, width), lambda i: (0, 0))

    def full(shape):
        return pl.BlockSpec(shape, lambda i: (0,) * len(shape))

    in_specs = [cur(c, COL_R), cur(c, COL_K), cur(c, COL_V), cur(MISC_WIDTH, COL_MISC),
                prev(c, COL_R), prev(c, COL_K), prev(c, COL_V), prev(MISC_WIDTH, COL_MISC),
                vec(c), vec(c), vec(c), vec(MISC_WIDTH), vec(c), vec(c), vec(c), vec(c), vec(c),
                full((MISC_WIDTH, c)), full((MISC_WIDTH, c)), full((LANES, LANES))]
    args = [proj, proj, proj, proj, proj, proj, proj, proj,
            p['mu_r'], p['mu_k'], p['mu_v'], p['mu_m'], p['w0'], p['a0'], p['k_k'], p['k_a'], p['r_k'],
            p['w_up'], p['a_up'], p['ones_bd']]
    if has_vres:
        in_specs += [vec(c), full((c, LANES)), full((LANES, c)),
                     pl.BlockSpec((tm, c), lambda i: (i, 0))]
        args += [vres['v0'], vres['v_dn'], vres['v_up'], v_first]
    out_spec = pl.BlockSpec((tm, c), lambda i: (i, 0))
    out_sds = jax.ShapeDtypeStruct((t, c), F32)
    return pl.pallas_call(
        functools.partial(_rwkv_prep_kernel, has_vres, seq // tm),
        grid=(t // tm,),
        in_specs=in_specs,
        out_specs=[out_spec] * 7,
        out_shape=[out_sds] * 7,
        compiler_params=_params(("parallel",)),
    )(*args)


def _wkv_kernel(r_ref, lw_ref, k_ref, v_ref, kk_ref, b_ref, bonus_ref, za_ref, gnw_ref, gnb_ref,
                ones_ref, o_ref, st_ref):
    nb, L, width = r_ref.shape
    W = HEAD_GROUP_LANES
    n_groups = width // W

    @pl.when(pl.program_id(1) == 0)
    def _():
        st_ref[...] = jnp.zeros_like(st_ref)

    ti = lax.broadcasted_iota(jnp.int32, (L, L), 0)
    tj = lax.broadcasted_iota(jnp.int32, (L, L), 1)
    tri = jnp.where(ti >= tj, 1.0, 0.0).astype(BF16)
    ri = lax.broadcasted_iota(jnp.int32, (W, W), 0)
    ci = lax.broadcasted_iota(jnp.int32, (W, W), 1)
    same_head = _head_of(ri) == _head_of(ci)
    incl = same_head & (ri >= ci)
    strict = same_head & (ri > ci)
    eye = jnp.where(ri == ci, 1.0, 0.0)

    def stack4(a):
        return jnp.concatenate([a] * 4, axis=0)

    def xform(a):
        return jnp.where(same_head, stack4(a), 0.0).astype(BF16)

    abx, rbx, vx, kpx, bpx, bk, pcol = [], [], [], [], [], [], []
    for bi in range(nb):
        lw = lw_ref[bi]
        lw_hi, lw_lo = _split2(lw)
        cum = _dot(tri, lw_hi) + _dot(tri, lw_lo)
        pend = cum[L - 1:L, :]
        e_n = jnp.exp(-cum)
        e_e = jnp.exp(pend - cum)
        b = b_ref[bi]
        k = k_ref[bi]
        ab_all = -(jnp.exp(cum - lw) * kk_ref[bi])
        rb_all = jnp.exp(cum) * r_ref[bi]
        bb_all = e_n * b
        kb_all = e_n * k
        bp_all = e_e * b
        kp_all = e_e * k
        v_all = v_ref[bi]
        p_end = jnp.exp(pend)
        for g in range(n_groups):
            sl = slice(g * W, (g + 1) * W)
            abx.append(xform(ab_all[:, sl]))
            rbx.append(xform(rb_all[:, sl]))
            vx.append(xform(v_all[:, sl]))
            kpx.append(xform(kp_all[:, sl]))
            bpx.append(xform(bp_all[:, sl]))
            bk.append(jnp.concatenate([stack4(bb_all[:, sl]), stack4(kb_all[:, sl])], axis=0).astype(BF16))
            pcol.append(jnp.transpose(jnp.broadcast_to(p_end[:, sl], (W, W))))
    chains = range(nb * n_groups)

    g_a = [_dot_nt(abx[c], bk[c]) for c in chains]
    g_r = [_dot_nt(rbx[c], bk[c]) for c in chains]
    nm = [jnp.where(strict, g_a[c][:, 0:W], 0.0) for c in chains]
    ak = [jnp.where(strict, g_a[c][:, W:2 * W], 0.0).astype(BF16) for c in chains]
    rbm = [jnp.where(incl, g_r[c][:, 0:W], 0.0).astype(BF16) for c in chains]
    rkm = [jnp.where(incl, g_r[c][:, W:2 * W], 0.0).astype(BF16) for c in chains]

    tinv = [eye + nm[c] for c in chains]
    pw = [nm[c].astype(BF16) for c in chains]
    for _ in range(5):
        pw = [_dot(pw[c], pw[c]).astype(BF16) for c in chains]
        tinv = [tinv[c] + _dot(pw[c], tinv[c].astype(BF16)) for c in chains]
    tb = [tinv[c].astype(BF16) for c in chains]

    akv = [_dot(ak[c], vx[c]).astype(BF16) for c in chains]
    rkv = [_dot(rkm[c], vx[c]) for c in chains]
    kv = [_dot_tn(kpx[c], vx[c]) for c in chains]
    wa = [_dot(tb[c], abx[c]).astype(BF16) for c in chains]
    uv = [_dot(tb[c], akv[c]) for c in chains]

    st = [st_ref[c] for c in chains]
    stb = [st[c].astype(BF16) for c in chains]
    ub = [(_dot(wa[c], stb[c]) + uv[c]).astype(BF16) for c in chains]
    for c in chains:
        st_ref[c] = pcol[c] * st[c] + _dot_tn(bpx[c], ub[c]) + kv[c]
    yx = [_dot(rbx[c], stb[c]) + _dot(rbm[c], ub[c]) + rkv[c] for c in chains]

    ones_bd = ones_ref[...]
    inv_n = 1.0 / HEAD_DIM
    for bi in range(nb):
        parts = []
        for g in range(n_groups):
            y4 = yx[bi * n_groups + g]
            parts.append(y4[0:L] + y4[L:2 * L] + y4[2 * L:3 * L] + y4[3 * L:4 * L])
        y = jnp.concatenate(parts, axis=1)
        mean = _head_sum(y, ones_bd) * inv_n
        d = y - mean
        var = _head_sum(d * d, ones_bd) * inv_n
        yn = d * lax.rsqrt(var + GN_EPS) * gnw_ref[...] + gnb_ref[...]
        za = za_ref[bi]
        o_ref[bi] = ((yn + bonus_ref[bi]) * (za * _sigmoid(za))).astype(BF16)


def _wkv(prep, proj, seq, p, nb=WKV_BATCH):
    t, c = prep[0].shape
    batch = t // seq
    nb = min(nb, batch)
    n_chunks = seq // WKV_CHUNK
    as3d = lambda a: a.reshape(batch, seq, a.shape[1])
    blk = pl.BlockSpec((nb, WKV_CHUNK, c), lambda bi, ci: (bi, ci, 0))
    za = pl.BlockSpec((nb, WKV_CHUNK, c), lambda bi, ci: (bi, ci, COL_ZA // c))
    vec = pl.BlockSpec((1, c), lambda bi, ci: (0, 0))
    out = pl.pallas_call(
        _wkv_kernel,
        grid=(batch // nb, n_chunks),
        in_specs=[blk] * 7 + [za, vec, vec, pl.BlockSpec((LANES, LANES), lambda bi, ci: (0, 0))],
        out_specs=blk,
        out_shape=jax.ShapeDtypeStruct((batch, seq, c), BF16),
        scratch_shapes=[pltpu.VMEM((nb * c // HEAD_GROUP_LANES, HEAD_GROUP_LANES, HEAD_GROUP_LANES), F32)],
        compiler_params=_params(("parallel", "arbitrary")),
    )(*[as3d(a) for a in prep], as3d(proj), p['gn_w'], p['gn_b'], p['ones_bd'])
    return out.reshape(t, c)


def _rotary(x, cos_t, sin_up, sin_dn):
    reps = x.shape[1] // LANES
    half = ROT_DIM // 2
    tile = lambda tbl: jnp.concatenate([tbl] * reps, axis=1) if reps > 1 else tbl
    return (x * tile(cos_t) + pltpu.roll(x, half, axis=1) * tile(sin_up)
            + pltpu.roll(x, x.shape[1] - half, axis=1) * tile(sin_dn))


def _nsa_prep_kernel(q_ref, kv_ref, cos_ref, sup_ref, sdn_ref, sel_ref, selt_ref,
                     q_o, ks_o, vst_o, kw_o, vwt_o):
    cos_t, sin_up, sin_dn = cos_ref[...], sup_ref[...], sdn_ref[...]
    q = (_rotary(q_ref[...], cos_t, sin_up, sin_dn) * (HEAD_DIM ** -0.5 * LOG2E)).astype(BF16)
    w = NSA_KV_WIDTH
    kv = kv_ref[...]
    ks = _rotary(kv[:, 2 * w:3 * w], cos_t, sin_up, sin_dn).astype(BF16)
    vs = kv[:, 3 * w:4 * w].astype(BF16)
    kw = _rotary(kv[:, 4 * w:5 * w], cos_t, sin_up, sin_dn).astype(BF16)
    vw = kv[:, 5 * w:6 * w].astype(BF16)
    ones = jnp.ones((V_AUG_ROWS, q.shape[0]), BF16)
    for g in range(NSA_GROUPS):
        qg = q[:, g * MXU_DIM:(g + 1) * MXU_DIM]
        for h in range(HEADS_PER_GROUP):
            q_o[g, h] = _dot(qg, sel_ref[h]).astype(BF16)
        ks_o[g] = _dot(ks, sel_ref[g]).astype(BF16)
        kw_o[g] = _dot(kw, sel_ref[g]).astype(BF16)
        vst_o[g] = jnp.concatenate([_dot_nt(selt_ref[g], vs).astype(BF16), ones], axis=0)
        vwt_o[g] = jnp.concatenate([_dot_nt(selt_ref[g], vw).astype(BF16), ones], axis=0)


def _nsa_prep(proj, batch, seq, tables, sel, sel_t, tr=256):
    n_t = seq // tr
    row = lambda bi, i: bi * n_t + i
    tbl = pl.BlockSpec((tr, LANES), lambda bi, i: (row(bi, i), 0))
    g, h, d = NSA_GROUPS, HEADS_PER_GROUP, HEAD_DIM
    k_o = pl.BlockSpec((None, g, tr, d), lambda bi, i: (bi, 0, i, 0))
    vt_o = pl.BlockSpec((None, g, d + V_AUG_ROWS, tr), lambda bi, i: (bi, 0, 0, i))
    k_sds = jax.ShapeDtypeStruct((batch, g, seq, d), BF16)
    vt_sds = jax.ShapeDtypeStruct((batch, g, d + V_AUG_ROWS, seq), BF16)
    return pl.pallas_call(
        _nsa_prep_kernel,
        grid=(batch, n_t),
        in_specs=[
            pl.BlockSpec((tr, NSA_WIDTH), lambda bi, i: (row(bi, i), COL_Q // NSA_WIDTH)),
            pl.BlockSpec((tr, 6 * NSA_KV_WIDTH), lambda bi, i: (row(bi, i), COL_KV // (6 * NSA_KV_WIDTH))),
            tbl, tbl, tbl,
            pl.BlockSpec(sel.shape, lambda bi, i: (0, 0, 0)),
            pl.BlockSpec(sel_t.shape, lambda bi, i: (0, 0, 0)),
        ],
        out_specs=[pl.BlockSpec((None, g, h, tr, d), lambda bi, i: (bi, 0, 0, i, 0)), k_o, vt_o, k_o, vt_o],
        out_shape=[jax.ShapeDtypeStruct((batch, g, h, seq, d), BF16), k_sds, vt_sds, k_sds, vt_sds],
        compiler_params=_params(("parallel", "parallel")),
    )(proj, proj, tables[0], tables[1], tables[2], sel, sel_t)


def _compress_kernel(n_cmp, kc_ref, vc_ref, pek_ref, pev_ref, kw1_ref, kw2_ref, vw1_ref, vw2t_ref,
                     cos_ref, sup_ref, sdn_ref, ko_ref, vto_ref):
    half = kw1_ref.shape[0] // 2
    rows = kc_ref.shape[0]

    def hidden(c_ref, pe_ref, w1_ref):
        c = c_ref[...].astype(BF16)
        pe = jnp.broadcast_to(pe_ref[...], (8, 2 * half)).astype(BF16)
        h = (_dot(c, w1_ref[0:half, :])
             + pltpu.roll(_dot(c, w1_ref[half:2 * half, :]), rows - 1, axis=0)
             + _dot(pe, w1_ref[...])[0:1, :])
        inner = math.sqrt(2.0 / math.pi) * (h + 0.044715 * (h * h * h))
        return (0.5 * h * (1.0 + jnp.tanh(inner))).astype(BF16)

    kc = _rotary(_dot(hidden(kc_ref, pek_ref, kw1_ref), kw2_ref[...]),
                 cos_ref[...], sup_ref[...], sdn_ref[...])
    valid_row = lax.broadcasted_iota(jnp.int32, (rows, 1), 0) < n_cmp
    ko_ref[...] = jnp.where(valid_row, kc[:, 0:HEAD_DIM], 0.0).astype(BF16)
    vct = _dot_nt(vw2t_ref[...], hidden(vc_ref, pev_ref, vw1_ref))
    valid_col = lax.broadcasted_iota(jnp.int32, (1, rows), 1) < n_cmp
    vto_ref[...] = jnp.where(valid_col, vct, 0.0).astype(BF16)


def _compress(kc_in, vc_in, p, cmp_tables, n_cmp):
    bg, rows, width = kc_in.shape
    groups = NSA_GROUPS
    cin = pl.BlockSpec((None, rows, width), lambda i: (i, 0, 0))
    full = lambda shape: pl.BlockSpec(shape, lambda i: (0,) * len(shape))
    tbl = pl.BlockSpec((None, rows, LANES), lambda i: (i // groups, 0, 0))
    return pl.pallas_call(
        functools.partial(_compress_kernel, n_cmp),
        grid=(bg,),
        in_specs=[cin, cin, full((1, 2 * width)), full((1, 2 * width)),
                  full((2 * width, CMP_HIDDEN)), full((CMP_HIDDEN, LANES)),
                  full((2 * width, CMP_HIDDEN)), full((HEAD_DIM, CMP_HIDDEN)),
                  tbl, tbl, tbl],
        out_specs=[pl.BlockSpec((None, rows, HEAD_DIM), lambda i: (i, 0, 0)),
                   pl.BlockSpec((None, HEAD_DIM, rows), lambda i: (i, 0, 0))],
        out_shape=[jax.ShapeDtypeStruct((bg, rows, HEAD_DIM), BF16),
                   jax.ShapeDtypeStruct((bg, HEAD_DIM, rows), BF16)],
        compiler_params=_params(("parallel",)),
    )(kc_in, vc_in, p['pe_k'], p['pe_v'], p['ck_w1'], p['ck_w2'], p['cv_w1'], p['cv_w2t'], *cmp_tables)


def _mask_block_rows(s, allowed, tq):
    heads = s.shape[1] // tq
    return jnp.concatenate(
        [jnp.where(allowed, s[:, h * tq:(h + 1) * tq], MASK_VALUE) for h in range(heads)], axis=1)


def _nsa_attn_kernel(n_cmp, sel_k, q_ref, kc_ref, vct_ref, ks_ref, vst_ref, kw_ref, vwt_ref,
                     gate_ref, zb_ref, ovt_ref, o_ref, thr_ref, gt_ref, sa_ref, sb_ref, ea_ref, eb_ref,
                     sw_ref):
    H = HEADS_PER_GROUP
    D = HEAD_DIM
    tq = q_ref.shape[1]
    nl = H * tq
    n_blk = ovt_ref.shape[0]
    grp = pl.program_id(1)
    t0 = pl.program_id(2) * tq

    qs = q_ref[...].reshape(nl, D)
    tq_pos = t0 + lax.broadcasted_iota(jnp.int32, (1, tq), 1)

    tk = sa_ref.shape[0]
    wlen = sw_ref.shape[0]
    wstart = pl.multiple_of(jnp.maximum(t0 + tq - wlen, 0), tq)

    def scores(kt):
        return _dot_nt(ks_ref[pl.ds(pl.multiple_of(kt * tk, tk), tk), :], qs)

    s_cmp = _dot_nt(kc_ref[...], qs)
    sw_ref[...] = _dot_nt(kw_ref[pl.ds(wstart, wlen), :], qs)
    sa_ref[...] = scores(0)

    n_pad = kc_ref.shape[0]
    crow = lax.broadcasted_iota(jnp.int32, (n_pad, tq), 0)
    cmp_ok = (crow * CMP_STRIDE + (CMP_LEN - 1) <= tq_pos) & (crow < n_cmp)
    s = _mask_block_rows(s_cmp, cmp_ok, tq)
    e = jnp.exp2(s - jnp.max(s, axis=0, keepdims=True))
    has_block = jnp.concatenate([tq_pos >= CMP_LEN - 1] * H, axis=1)
    p = e * jnp.where(has_block, 1.0 / jnp.sum(e, axis=0, keepdims=True), 0.0)
    o_cmp = _dot(vct_ref[...], p.astype(BF16))
    psum = p[:, 0:tq]
    for h in range(1, H):
        psum = psum + p[:, h * tq:(h + 1) * tq]

    p_hi, p_lo = _split2(psum)
    ovt = ovt_ref[...]
    imp = _dot(ovt, p_hi) + _dot(ovt, p_lo)
    blk = lax.broadcasted_iota(jnp.int32, (n_blk, tq), 0)
    tlane = t0 + lax.broadcasted_iota(jnp.int32, (n_blk, tq), 1)
    dist = _head_of(tlane) - blk
    forced = (blk == 0) | ((dist >= 0) & (dist < N_LOCAL_SEL))
    causal_blk = blk * SEL_LEN <= tlane
    imp = jnp.where(causal_blk, imp + jnp.where(forced, FORCE_BONUS, 0.0), MASK_VALUE)
    sub = 8
    row_in_group = lax.broadcasted_iota(jnp.int32, (sub, tq), 0)
    groups = [imp[v * sub:(v + 1) * sub, :] for v in range(n_blk // sub)]
    rank = [jnp.zeros((sub, tq), F32) for _ in groups]
    for j in range(n_blk):
        rj = imp[j:j + 1, :]
        for v, blk_imp in enumerate(groups):
            if v * sub > j:
                ahead = jnp.where(rj >= blk_imp, 1.0, 0.0)
            elif (v + 1) * sub <= j:
                ahead = jnp.where(rj > blk_imp, 1.0, 0.0)
            else:
                ahead = jnp.where(row_in_group > j - v * sub,
                                  jnp.where(rj >= blk_imp, 1.0, 0.0), jnp.where(rj > blk_imp, 1.0, 0.0))
            rank[v] = rank[v] + ahead
    rank = jnp.concatenate(rank, axis=0)
    thr_ref[...] = jnp.where(rank < sel_k, (tlane - blk * SEL_LEN).astype(F32), -1.0)

    blocks_per_tile = tk // SEL_LEN
    n_kt = (t0 + tq + tk - 1) // tk
    last = n_kt - 1
    r_in_blk = lax.broadcasted_iota(jnp.int32, (SEL_LEN, tq), 0).astype(F32)

    def weighted_values(kt, e_buf):
        return _dot(vst_ref[:, pl.ds(pl.multiple_of(kt * tk, tk), tk)], e_buf[...])

    def softmax_step(kt, valid, s_buf, e_buf, m):
        rows = []
        for j in range(blocks_per_tile):
            limit = thr_ref[pl.ds(kt * blocks_per_tile + j, 1), :]
            if valid is not None:
                limit = jnp.where(valid, limit, -1.0)
            rows.append(_mask_block_rows(s_buf[j * SEL_LEN:(j + 1) * SEL_LEN, :], r_in_blk <= limit, tq))
        s = jnp.concatenate(rows, axis=0)
        m_new = jnp.maximum(m, jnp.max(s, axis=0, keepdims=True))
        e_buf[...] = jnp.exp2(s - m_new).astype(BF16)
        return m_new, jnp.exp2(m - m_new)

    eb_ref[...] = jnp.zeros_like(eb_ref)

    def body(i, carry):
        m, acc, alpha_b = carry
        kt_a = 2 * i
        kt_b = kt_a + 1
        pv_b = weighted_values(jnp.maximum(kt_a - 1, 0), eb_ref)
        sb_ref[...] = scores(jnp.minimum(kt_b, last))
        m, alpha_a = softmax_step(kt_a, None, sa_ref, ea_ref, m)
        acc = alpha_b * acc + pv_b
        pv_a = weighted_values(kt_a, ea_ref)
        sa_ref[...] = scores(jnp.minimum(kt_a + 2, last))
        m, alpha_b = softmax_step(jnp.minimum(kt_b, last), kt_b <= last, sb_ref, eb_ref, m)
        acc = alpha_a * acc + pv_a
        return m, acc, alpha_b

    n_pairs = (n_kt + 1) // 2
    init = (jnp.full((1, nl), MASK_VALUE, F32), jnp.zeros((D + V_AUG_ROWS, nl), F32), jnp.zeros((1, nl), F32))
    _, acc, alpha_b = lax.fori_loop(0, n_pairs, body, init)
    acc = alpha_b * acc + weighted_values(jnp.minimum(2 * n_pairs - 1, last), eb_ref)
    o_slc = acc[0:D] * (1.0 / acc[D:D + 1])

    rel = tq_pos - wstart
    wrow = lax.broadcasted_iota(jnp.int32, (wlen, tq), 0)
    head_rows = (wrow[0:tq] <= rel) & (wrow[0:tq] > rel - WINDOW)
    s = jnp.concatenate([_mask_block_rows(sw_ref[0:tq, :], head_rows, tq),
                         _mask_block_rows(sw_ref[tq:wlen, :], wrow[tq:wlen] <= rel, tq)], axis=0)
    e = jnp.exp2(s - jnp.max(s, axis=0, keepdims=True)).astype(BF16)
    acc = _dot(vwt_ref[:, pl.ds(wstart, wlen)], e)
    o_win = acc[0:D] * (1.0 / acc[D:D + 1])

    gt_ref[...] = jnp.transpose(_sigmoid(gate_ref[...]))

    def gate_row(branch):
        base = (GATE_OFF - LANES) + branch * NSA_HEADS + grp * H
        return jnp.concatenate([gt_ref[pl.ds(base + h, 1), :] for h in range(H)], axis=1)

    mix = gate_row(0) * o_cmp + gate_row(1) * o_slc + gate_row(2) * o_win
    nat = jnp.transpose(jnp.concatenate([mix[:, h * tq:(h + 1) * tq] for h in range(H)], axis=0))
    zb = zb_ref[...]
    o_ref[...] = (nat * (zb * _sigmoid(zb))).astype(BF16)


def _nsa_attn(q_heads, kc, vct, kv, proj, batch, seq, ovt, n_cmp, sel_k, tq=Q_TILE):
    t = batch * seq
    n_q = seq // tq
    assert tq == LANES, "the gate block is transposed as one 128 x 128 tile"
    tk = KEY_TILE if seq >= 2 * KEY_TILE else seq // 2
    wlen = WINDOW + tq if seq >= WINDOW + tq else seq
    ks, vst, kw, vwt = kv
    g, h, d = NSA_GROUPS, HEADS_PER_GROUP, HEAD_DIM
    row = lambda b, gi, i: b * n_q + i
    whole = lambda a: pl.BlockSpec((None, None) + a.shape[2:], lambda b, gi, i: (b, gi, 0, 0))
    kc = kc.reshape((batch, g) + kc.shape[1:])
    vct = vct.reshape((batch, g) + vct.shape[1:])
    return pl.pallas_call(
        functools.partial(_nsa_attn_kernel, n_cmp, sel_k),
        grid=(batch, g, n_q),
        in_specs=[
            pl.BlockSpec((None, None, h, tq, d), lambda b, gi, i: (b, gi, 0, i, 0)),
            whole(kc), whole(vct), whole(ks), whole(vst), whole(kw), whole(vwt),
            pl.BlockSpec((tq, LANES), lambda b, gi, i: (row(b, gi, i), (COL_MISC + LANES) // LANES)),
            pl.BlockSpec((tq, MXU_DIM), lambda b, gi, i: (row(b, gi, i), COL_ZB // MXU_DIM + gi)),
            pl.BlockSpec(ovt.shape, lambda b, gi, i: (0, 0)),
        ],
        out_specs=pl.BlockSpec((tq, MXU_DIM), lambda b, gi, i: (row(b, gi, i), gi)),
        out_shape=jax.ShapeDtypeStruct((t, NSA_WIDTH), BF16),
        scratch_shapes=[pltpu.VMEM((ovt.shape[0], tq), F32), pltpu.VMEM((LANES, tq), F32),
                        pltpu.VMEM((tk, h * tq), F32), pltpu.VMEM((tk, h * tq), F32),
                        pltpu.VMEM((tk, h * tq), BF16), pltpu.VMEM((tk, h * tq), BF16),
                        pltpu.VMEM((wlen, h * tq), F32)],
        compiler_params=_params(("parallel", "parallel", "arbitrary")),
    )(q_heads, kc, vct, ks, vst, kw, vwt, proj, proj, ovt)


def _merge_kernel(ya_ref, yb_ref, wa_ref, wb_ref, ga_ref, gb_ref, o_ref):
    pa = _dot(ya_ref[...], wa_ref[...])
    pb = _dot(yb_ref[...], wb_ref[...])
    o_ref[...] = (_sigmoid(ga_ref[...]) * pa + _sigmoid(gb_ref[...]) * pb).astype(BF16)


def _merge(y_a, y_b, w_a, w_b, proj, layer, tm=512):
    t, c = y_a.shape
    yspec = pl.BlockSpec((tm, c), lambda i: (i, 0))
    wspec = pl.BlockSpec((None, c, D_MODEL), lambda i: (layer, 0, 0))
    return pl.pallas_call(
        _merge_kernel,
        grid=(t // tm,),
        in_specs=[yspec, yspec, wspec, wspec,
                  pl.BlockSpec((tm, D_MODEL), lambda i: (i, COL_MERGE // D_MODEL)),
                  pl.BlockSpec((tm, D_MODEL), lambda i: (i, COL_MERGE // D_MODEL + 1))],
        out_specs=pl.BlockSpec((tm, D_MODEL), lambda i: (i, 0)),
        out_shape=jax.ShapeDtypeStruct((t, D_MODEL), BF16),
        compiler_params=_params(("parallel",)),
    )(y_a, y_b, w_a, w_b, proj, proj)


def _outproj_kernel(m_ref, w_ref, g_ref, x_ref, o_ref):
    out = _dot(m_ref[...], w_ref[...])
    ms = jnp.mean(out * out, axis=-1, keepdims=True)
    o_ref[...] = x_ref[...] + out * lax.rsqrt(ms + NORM_EPS) * g_ref[...]


def _outproj(merged, w_out, gain, x, layer, tm=256):
    t = x.shape[0]
    rows = pl.BlockSpec((tm, D_MODEL), lambda i: (i, 0))
    return pl.pallas_call(
        _outproj_kernel,
        grid=(t // tm,),
        in_specs=[rows, pl.BlockSpec((None, D_MODEL, D_MODEL), lambda i: (layer, 0, 0)),
                  pl.BlockSpec((1, D_MODEL), lambda i: (0, 0)), rows],
        out_specs=rows,
        out_shape=jax.ShapeDtypeStruct((t, D_MODEL), F32),
        compiler_params=_params(("parallel",)),
    )(merged, w_out, gain, x)


def _reorder_w_in(w):
    u_end = SHIFT_WIDTH
    za = u_end
    q = za + RWKV_WIDTH
    kv = q + NSA_WIDTH
    gate = kv + 6 * NSA_KV_WIDTH
    zb = gate + 3 * NSA_HEADS
    merge = zb + NSA_WIDTH
    end = merge + 2 * D_MODEL
    lead = w.shape[:-1]
    pieces = [w[..., merge:end], w[..., 0:3 * RWKV_WIDTH], w[..., za:q], w[..., q:kv], w[..., kv:gate],
              w[..., zb:merge], w[..., 3 * RWKV_WIDTH:u_end], w[..., gate:zb],
              jnp.zeros(lead + (MISC_WIDTH - GATE_OFF - 3 * NSA_HEADS,), w.dtype),
              jnp.zeros(lead + (PROJ_WIDTH - COL_MISC - MISC_WIDTH,), w.dtype)]
    return jnp.concatenate([p.astype(BF16) for p in pieces], axis=-1)


def _rope_tables(pos):
    half = ROT_DIM // 2
    inv_freq = ROPE_THETA ** (-jnp.arange(half, dtype=F32) * (2.0 / ROT_DIM))
    ang = pos.astype(F32)[..., None] * inv_freq
    cos, sin = jnp.cos(ang), jnp.sin(ang)
    d = np.arange(LANES) % HEAD_DIM
    idx = d % half
    cos_l, sin_l = cos[..., idx], sin[..., idx]
    cos_t = jnp.where(d < ROT_DIM, cos_l, 1.0)
    sin_up = jnp.where((d >= half) & (d < ROT_DIM), sin_l, 0.0)
    sin_dn = jnp.where(d < half, -sin_l, 0.0)
    return cos_t, sin_up, sin_dn


def _constants(seq):
    n_cmp = (seq - CMP_LEN) // CMP_STRIDE + 1
    n_sel = seq // SEL_LEN
    n_pad = seq // CMP_STRIDE
    ones_bd = (np.arange(LANES)[:, None] // HEAD_DIM == np.arange(LANES)[None, :] // HEAD_DIM)
    sel = np.stack([np.arange(MXU_DIM)[:, None] == i * HEAD_DIM + np.arange(HEAD_DIM)[None, :]
                    for i in range(NSA_GROUPS)])
    ci = np.arange(n_pad)[:, None] * CMP_STRIDE
    sj = np.arange(n_sel)[None, :] * SEL_LEN
    overlap = (ci < sj + SEL_LEN) & (ci + CMP_LEN > sj) & (np.arange(n_pad)[:, None] < n_cmp)
    to_bf = lambda a: jnp.asarray(a.astype(np.float32), BF16)
    return dict(ones_bd=to_bf(ones_bd), sel=to_bf(sel), sel_t=to_bf(sel.transpose(0, 2, 1)),
                overlap_t=to_bf(overlap.T)), n_cmp, n_sel


def _pad_rows(w, rows, at):
    out = jnp.zeros((rows, w.shape[1]), w.dtype)
    return out.at[at:at + w.shape[0]].set(w)


def kernel(x, positions, norm_pre, norm_post, w_in, rwkv_mu, rwkv_w0, rwkv_w_up, rwkv_a0, rwkv_a_up,
           rwkv_k_k, rwkv_k_a, rwkv_r_k, rwkv_gn_w, rwkv_gn_b, rwkv_v0, rwkv_v_down, rwkv_v_up,
           nsa_pe_k, nsa_pe_v, nsa_ck_w1, nsa_ck_w2, nsa_cv_w1, nsa_cv_w2, w_proj_a, w_proj_b, w_out):
    batch, seq, d_model = x.shape
    depth = w_in.shape[0]
    t = batch * seq
    c = RWKV_WIDTH
    consts, n_cmp, n_sel = _constants(seq)
    sel_k = min(SEL_TOPK, n_sel)
    n_pad = seq // CMP_STRIDE

    tables = _rope_tables(positions.reshape(t))
    cmp_pos = positions[:, CMP_LEN - 1::CMP_STRIDE]
    cmp_pos = jnp.concatenate([cmp_pos, cmp_pos[:, -1:]], axis=1)
    cmp_tables = _rope_tables(cmp_pos)

    row = lambda a: a.reshape(1, -1).astype(F32)
    xf = x.reshape(t, d_model)
    v_first = None
    w_in_b = _reorder_w_in(w_in)
    w_a_b, w_b_b, w_out_b = w_proj_a.astype(BF16), w_proj_b.astype(BF16), w_out.astype(BF16)
    for l in range(depth):
        proj = _inproj(xf, row(norm_pre[l]), w_in_b, l)

        mu = rwkv_mu[l]
        rp = dict(
            mu_r=row(mu[0:c]), mu_k=row(mu[c:2 * c]), mu_v=row(mu[2 * c:3 * c]),
            mu_m=row(jnp.concatenate([mu[3 * c:], jnp.zeros((MISC_WIDTH - GATE_OFF,), F32)])),
            w0=row(rwkv_w0[l]), a0=row(rwkv_a0[l]), k_k=row(rwkv_k_k[l]), k_a=row(rwkv_k_a[l]),
            r_k=row(rwkv_r_k[l]),
            w_up=_pad_rows(rwkv_w_up[l], MISC_WIDTH, 0).astype(BF16),
            a_up=_pad_rows(rwkv_a_up[l], MISC_WIDTH, DECAY_LORA).astype(BF16),
            ones_bd=consts['ones_bd'], gn_w=row(rwkv_gn_w[l]), gn_b=row(rwkv_gn_b[l]))
        vres = None
        if l > 0:
            vres = dict(v0=row(rwkv_v0[l - 1]),
                        v_dn=jnp.pad(rwkv_v_down[l - 1], ((0, 0), (0, LANES - VRES_LORA))).astype(BF16),
                        v_up=_pad_rows(rwkv_v_up[l - 1], LANES, 0).astype(BF16))
        prep = _rwkv_prep(proj, seq, rp, vres, v_first)
        if l == 0:
            v_first = prep[3]
        y_a = _wkv(prep, proj, seq, rp)

        q_heads, ks, vst, kw, vwt = _nsa_prep(proj, batch, seq, tables, consts['sel'], consts['sel_t'])
        kv = proj[:, COL_KV:COL_KV + 2 * NSA_KV_WIDTH].reshape(batch, seq, 2, NSA_GROUPS, HEAD_DIM)
        kv = kv.transpose(2, 0, 3, 1, 4).reshape(2, batch * NSA_GROUPS, n_pad, CMP_STRIDE * HEAD_DIM)
        cp = dict(pe_k=nsa_pe_k[l].reshape(1, -1), pe_v=nsa_pe_v[l].reshape(1, -1),
                  ck_w1=nsa_ck_w1[l].astype(BF16),
                  ck_w2=jnp.tile(nsa_ck_w2[l], (1, LANES // HEAD_DIM)).astype(BF16),
                  cv_w1=nsa_cv_w1[l].astype(BF16), cv_w2t=nsa_cv_w2[l].T.astype(BF16))
        kc, vct = _compress(kv[0], kv[1], cp, cmp_tables, n_cmp)
        y_b = _nsa_attn(q_heads, kc, vct, (ks, vst, kw, vwt), proj, batch, seq, consts['overlap_t'],
                        n_cmp, sel_k)

        merged = _merge(y_a, y_b, w_a_b, w_b_b, proj, l)
        xf = _outproj(merged, w_out_b, row(norm_post[l]), xf, l)
    return xf.reshape(batch, seq, d_model)
```

```python
import functools
import math

import numpy as np
import jax
import jax.numpy as jnp
from jax import lax
from jax.experimental import pallas as pl
from jax.experimental.pallas import tpu as pltpu

F32 = jnp.float32
BF16 = jnp.bfloat16

D_MODEL = 2048
RWKV_HEADS = 16
HEAD_DIM = 64
RWKV_WIDTH = RWKV_HEADS * HEAD_DIM
DECAY_LORA = 96
ICLR_LORA = 96
VRES_LORA = 64
GN_EPS = 64e-5
L2_EPS = 1e-12
SHIFT_WIDTH = 3 * RWKV_WIDTH + DECAY_LORA + ICLR_LORA

NSA_HEADS = 16
NSA_GROUPS = 4
HEADS_PER_GROUP = NSA_HEADS // NSA_GROUPS
NSA_WIDTH = NSA_HEADS * HEAD_DIM
NSA_KV_WIDTH = NSA_GROUPS * HEAD_DIM
CMP_LEN = 32
CMP_STRIDE = 16
CMP_HIDDEN = 256
SEL_LEN = 64
SEL_TOPK = 16
N_LOCAL_SEL = 2
WINDOW = 512
ROPE_THETA = 500000.0
ROT_DIM = HEAD_DIM // 4
FORCE_BONUS = 1e3
MASK_VALUE = -1e30
NORM_EPS = 1e-6

LANES = 128
MXU_DIM = 256
VMEM_LIMIT = 56 * 1024 * 1024

COL_MERGE = 0
COL_R = 4096
COL_K = 5120
COL_V = 6144
COL_ZA = 7168
COL_Q = 8192
COL_KV = 9216
COL_ZB = 10752
COL_MISC = 11776
MISC_WIDTH = 256
GATE_OFF = DECAY_LORA + ICLR_LORA
PROJ_WIDTH = 12288

WKV_CHUNK = 64
WKV_BATCH = 2
HEAD_GROUP_LANES = 256
Q_TILE = 128
KEY_TILE = 256
V_AUG_ROWS = 16
LOG2E = 1.4426950408889634


def _dot(a, b):
    return jnp.dot(a, b, preferred_element_type=F32)


def _dot_nt(a, b):
    return lax.dot_general(a, b, (((1,), (1,)), ((), ())), preferred_element_type=F32)


def _dot_tn(a, b):
    return lax.dot_general(a, b, (((0,), (0,)), ((), ())), preferred_element_type=F32)


def _split2(x):
    hi = x.astype(BF16)
    lo = (x - hi.astype(F32)).astype(BF16)
    return hi, lo


def _dot2(x, w):
    hi, lo = _split2(x)
    return _dot(hi, w) + _dot(lo, w)


def _head_sum(x, ones_bd):
    width = ones_bd.shape[0]
    parts = []
    for c in range(x.shape[1] // width):
        parts.append(_dot2(x[:, c * width:(c + 1) * width], ones_bd))
    return jnp.concatenate(parts, axis=1)


def _sigmoid(x):
    return 1.0 / (1.0 + jnp.exp(-x))


def _head_of(idx):
    return jnp.right_shift(idx, int(math.log2(HEAD_DIM)))


def _params(sem):
    return pltpu.CompilerParams(dimension_semantics=sem, vmem_limit_bytes=VMEM_LIMIT)


def _inproj_kernel(x_ref, g_ref, w_ref, o_ref, h_ref):
    @pl.when(pl.program_id(1) == 0)
    def _():
        x = x_ref[...]
        ms = jnp.mean(x * x, axis=-1, keepdims=True)
        h_ref[...] = (x * lax.rsqrt(ms + NORM_EPS) * g_ref[...]).astype(BF16)

    o_ref[...] = _dot(h_ref[...], w_ref[...])


def _inproj(x, gain, w, layer, tm=1024, tn=1024):
    t = x.shape[0]
    return pl.pallas_call(
        _inproj_kernel,
        grid=(t // tm, PROJ_WIDTH // tn),
        in_specs=[
            pl.BlockSpec((tm, D_MODEL), lambda i, j: (i, 0)),
            pl.BlockSpec((1, D_MODEL), lambda i, j: (0, 0)),
            pl.BlockSpec((None, D_MODEL, tn), lambda i, j: (layer, 0, j)),
        ],
        out_specs=pl.BlockSpec((tm, tn), lambda i, j: (i, j)),
        out_shape=jax.ShapeDtypeStruct((t, PROJ_WIDTH), F32),
        scratch_shapes=[pltpu.VMEM((tm, D_MODEL), BF16)],
        compiler_params=_params(("parallel", "arbitrary")),
    )(x, gain, w)


def _wkv_kernel(has_vres, *refs):
    (r_ref, k_ref, v_ref, m_ref, za_ref, mu_r, mu_k, mu_v, mu_m, w0_ref, a0_ref, kk_w, ka_w, rk_w,
     wup_ref, aup_ref, gnw_ref, gnb_ref, ones_ref) = refs[:19]
    pos = 19
    if has_vres:
        v0_ref, vdn_ref, vup_ref, vf_ref = refs[pos:pos + 4]
        pos += 4
        o_ref = refs[pos]
        pos += 1
    else:
        o_ref, vf_out = refs[pos:pos + 2]
        pos += 2
    st_ref, lr_ref, lk_ref, lv_ref, lm_ref = refs[pos:pos + 5]

    nb, L, width = r_ref.shape
    W = HEAD_GROUP_LANES
    n_groups = width // W
    first_chunk = pl.program_id(1) == 0

    @pl.when(first_chunk)
    def _():
        st_ref[...] = jnp.zeros_like(st_ref)

    rows = nb * L
    ti = lax.broadcasted_iota(jnp.int32, (rows, rows), 0)
    tj = lax.broadcasted_iota(jnp.int32, (rows, rows), 1)
    tri = jnp.where((_head_of(ti) == _head_of(tj)) & (ti >= tj), 1.0, 0.0).astype(BF16)
    ri = lax.broadcasted_iota(jnp.int32, (W, W), 0)
    ci = lax.broadcasted_iota(jnp.int32, (W, W), 1)
    same_head = _head_of(ri) == _head_of(ci)
    incl = same_head & (ri >= ci)
    strict = same_head & (ri > ci)
    eye = jnp.where(ri == ci, 1.0, 0.0)
    row_id = lax.broadcasted_iota(jnp.int32, (rows, 1), 0)
    ones_bd = ones_ref[...]
    tail = lr_ref.shape[1]

    def stack4(a):
        return jnp.concatenate([a] * 4, axis=0)

    def xform(a):
        return jnp.where(same_head, stack4(a), 0.0).astype(BF16)

    def shifted(u_ref, last_ref, mu_ref):
        u = u_ref[...].reshape(rows, u_ref.shape[2])
        prev = pltpu.roll(u, 1, axis=0)
        for bi in range(nb):
            last = last_ref[bi, tail - 1:tail, :]
            last = jnp.where(first_chunk, jnp.zeros_like(last), last)
            prev = jnp.where(row_id == bi * L, last, prev)
            last_ref[bi] = u[(bi + 1) * L - tail:(bi + 1) * L, :]
        return u + (prev - u) * mu_ref[...]

    r = shifted(r_ref, lr_ref, mu_r)
    k = shifted(k_ref, lk_ref, mu_k)
    v = shifted(v_ref, lv_ref, mu_v)
    misc = shifted(m_ref, lm_ref, mu_m)
    z = -(w0_ref[...] + _dot(jnp.tanh(misc).astype(BF16), wup_ref[...]))
    softplus = jnp.maximum(z, 0.0) + jnp.log(1.0 + jnp.exp(-jnp.abs(z)))
    lw = -jnp.exp(-softplus - 0.5)
    a = _sigmoid(a0_ref[...] + _dot(misc.astype(BF16), aup_ref[...]))
    if has_vres:
        low = _dot(v.astype(BF16), vdn_ref[...])
        gate = _sigmoid(v0_ref[...] + _dot(low.astype(BF16), vup_ref[...]))
        v = v + (vf_ref[...].reshape(rows, width) - v) * gate
    else:
        vf_out[...] = v.reshape(nb, L, width)
    kk = k * kk_w[...]
    kk = kk / jnp.maximum(jnp.sqrt(_head_sum(kk * kk, ones_bd)), L2_EPS)
    k = k * (1.0 + (a - 1.0) * ka_w[...])
    b = kk * a
    bonus = _head_sum(r * k * rk_w[...], ones_bd) * v

    lw_hi, lw_lo = _split2(lw)
    cum = _dot(tri, lw_hi) + _dot(tri, lw_lo)
    ab_all = -(jnp.exp(cum - lw) * kk)
    rb_all = jnp.exp(cum) * r

    abx, rbx, vx, kpx, bpx, bk, pcol = [], [], [], [], [], [], []
    for bi in range(nb):
        rs = slice(bi * L, (bi + 1) * L)
        cum_b = cum[rs]
        pend = cum_b[L - 1:L, :]
        e_n = jnp.exp(-cum_b)
        e_e = jnp.exp(pend - cum_b)
        bb_all = e_n * b[rs]
        kb_all = e_n * k[rs]
        bp_all = e_e * b[rs]
        kp_all = e_e * k[rs]
        p_end = jnp.exp(pend)
        for g in range(n_groups):
            sl = slice(g * W, (g + 1) * W)
            abx.append(xform(ab_all[rs, sl]))
            rbx.append(xform(rb_all[rs, sl]))
            vx.append(xform(v[rs, sl]))
            kpx.append(xform(kp_all[:, sl]))
            bpx.append(xform(bp_all[:, sl]))
            bk.append(jnp.concatenate([stack4(bb_all[:, sl]), stack4(kb_all[:, sl])], axis=0).astype(BF16))
            pcol.append(jnp.transpose(jnp.broadcast_to(p_end[:, sl], (W, W))))
    chains = range(nb * n_groups)

    g_a = [_dot_nt(abx[c], bk[c]) for c in chains]
    g_r = [_dot_nt(rbx[c], bk[c]) for c in chains]
    nm = [jnp.where(strict, g_a[c][:, 0:W], 0.0) for c in chains]
    ak = [jnp.where(strict, g_a[c][:, W:2 * W], 0.0).astype(BF16) for c in chains]
    rbm = [jnp.where(incl, g_r[c][:, 0:W], 0.0).astype(BF16) for c in chains]
    rkm = [jnp.where(incl, g_r[c][:, W:2 * W], 0.0).astype(BF16) for c in chains]

    tinv = [eye + nm[c] for c in chains]
    pw = [nm[c].astype(BF16) for c in chains]
    for _ in range(5):
        pw = [_dot(pw[c], pw[c]).astype(BF16) for c in chains]
        tinv = [tinv[c] + _dot(pw[c], tinv[c].astype(BF16)) for c in chains]
    tb = [tinv[c].astype(BF16) for c in chains]

    akv = [_dot(ak[c], vx[c]).astype(BF16) for c in chains]
    rkv = [_dot(rkm[c], vx[c]) for c in chains]
    kv = [_dot_tn(kpx[c], vx[c]) for c in chains]
    wa = [_dot(tb[c], abx[c]).astype(BF16) for c in chains]
    uv = [_dot(tb[c], akv[c]) for c in chains]

    st = [st_ref[c] for c in chains]
    stb = [st[c].astype(BF16) for c in chains]
    ub = [(_dot(wa[c], stb[c]) + uv[c]).astype(BF16) for c in chains]
    for c in chains:
        st_ref[c] = pcol[c] * st[c] + _dot_tn(bpx[c], ub[c]) + kv[c]
    yx = [_dot(rbx[c], stb[c]) + _dot(rbm[c], ub[c]) + rkv[c] for c in chains]

    inv_n = 1.0 / HEAD_DIM
    per_elem = []
    for bi in range(nb):
        parts = []
        for g in range(n_groups):
            y4 = yx[bi * n_groups + g]
            parts.append(y4[0:L] + y4[L:2 * L] + y4[2 * L:3 * L] + y4[3 * L:4 * L])
        per_elem.append(jnp.concatenate(parts, axis=1))
    y = jnp.concatenate(per_elem, axis=0)
    mean = _head_sum(y, ones_bd) * inv_n
    d = y - mean
    var = _head_sum(d * d, ones_bd) * inv_n
    yn = d * lax.rsqrt(var + GN_EPS) * gnw_ref[...] + gnb_ref[...]
    za = za_ref[...].reshape(rows, width)
    o_ref[...] = ((yn + bonus) * (za * _sigmoid(za))).astype(BF16).reshape(nb, L, width)


def _wkv(proj, batch, seq, p, vres, v_first, nb=WKV_BATCH):
    c = RWKV_WIDTH
    nb = min(nb, batch)
    n_chunks = seq // WKV_CHUNK
    has_vres = vres is not None
    proj3 = proj.reshape(batch, seq, proj.shape[1])

    def cols(width, col):
        return pl.BlockSpec((nb, WKV_CHUNK, width), lambda bi, ci, _c=col // width: (bi, ci, _c))

    def full(shape):
        return pl.BlockSpec(shape, lambda bi, ci: (0,) * len(shape))

    vec = full((1, c))
    in_specs = [cols(c, COL_R), cols(c, COL_K), cols(c, COL_V), cols(MISC_WIDTH, COL_MISC), cols(c, COL_ZA),
                vec, vec, vec, full((1, MISC_WIDTH)), vec, vec, vec, vec, vec,
                full((MISC_WIDTH, c)), full((MISC_WIDTH, c)), vec, vec,
                full((HEAD_GROUP_LANES, HEAD_GROUP_LANES))]
    args = [proj3, proj3, proj3, proj3, proj3, p['mu_r'], p['mu_k'], p['mu_v'], p['mu_m'], p['w0'], p['a0'],
            p['k_k'], p['k_a'], p['r_k'], p['w_up'], p['a_up'], p['gn_w'], p['gn_b'], p['ones_bd']]
    rows = pl.BlockSpec((nb, WKV_CHUNK, c), lambda bi, ci: (bi, ci, 0))
    y_sds = jax.ShapeDtypeStruct((batch, seq, c), BF16)
    if has_vres:
        in_specs += [vec, full((c, LANES)), full((LANES, c)), rows]
        args += [vres['v0'], vres['v_dn'], vres['v_up'], v_first]
        out_specs, out_shape = rows, y_sds
    else:
        out_specs, out_shape = [rows, rows], [y_sds, jax.ShapeDtypeStruct((batch, seq, c), F32)]
    tail = 8
    out = pl.pallas_call(
        functools.partial(_wkv_kernel, has_vres),
        grid=(batch // nb, n_chunks),
        in_specs=in_specs,
        out_specs=out_specs,
        out_shape=out_shape,
        scratch_shapes=[pltpu.VMEM((nb * c // HEAD_GROUP_LANES, HEAD_GROUP_LANES, HEAD_GROUP_LANES), F32),
                        pltpu.VMEM((nb, tail, c), F32), pltpu.VMEM((nb, tail, c), F32),
                        pltpu.VMEM((nb, tail, c), F32), pltpu.VMEM((nb, tail, MISC_WIDTH), F32)],
        compiler_params=_params(("parallel", "arbitrary")),
    )(*args)
    if has_vres:
        return out.reshape(batch * seq, c), None
    return out[0].reshape(batch * seq, c), out[1]


def _rotary(x, cos_t, sin_up, sin_dn):
    reps = x.shape[1] // LANES
    half = ROT_DIM // 2
    tile = lambda tbl: jnp.concatenate([tbl] * reps, axis=1) if reps > 1 else tbl
    return (x * tile(cos_t) + pltpu.roll(x, half, axis=1) * tile(sin_up)
            + pltpu.roll(x, x.shape[1] - half, axis=1) * tile(sin_dn))


def _nsa_prep_kernel(q_ref, kv_ref, cos_ref, sup_ref, sdn_ref, sel_ref, selt_ref,
                     q_o, ks_o, vst_o, kw_o, vwt_o, kcc_o, vcc_o, stage_ref):
    cos_t, sin_up, sin_dn = cos_ref[...], sup_ref[...], sdn_ref[...]
    q = (_rotary(q_ref[...], cos_t, sin_up, sin_dn) * (HEAD_DIM ** -0.5 * LOG2E)).astype(BF16)
    w = NSA_KV_WIDTH
    kv = kv_ref[...]
    ks = _rotary(kv[:, 2 * w:3 * w], cos_t, sin_up, sin_dn).astype(BF16)
    vs = kv[:, 3 * w:4 * w].astype(BF16)
    kw = _rotary(kv[:, 4 * w:5 * w], cos_t, sin_up, sin_dn).astype(BF16)
    vw = kv[:, 5 * w:6 * w].astype(BF16)
    ones = jnp.ones((V_AUG_ROWS, q.shape[0]), BF16)
    for g in range(NSA_GROUPS):
        qg = q[:, g * MXU_DIM:(g + 1) * MXU_DIM]
        for h in range(HEADS_PER_GROUP):
            q_o[g, h] = _dot(qg, sel_ref[h]).astype(BF16)
        ks_o[g] = _dot(ks, sel_ref[g]).astype(BF16)
        kw_o[g] = _dot(kw, sel_ref[g]).astype(BF16)
        vst_o[g] = jnp.concatenate([_dot_nt(selt_ref[g], vs).astype(BF16), ones], axis=0)
        vwt_o[g] = jnp.concatenate([_dot_nt(selt_ref[g], vw).astype(BF16), ones], axis=0)
    n_stage = stage_ref.shape[0]
    for c in range(n_stage):
        stage_ref[c] = kv[:, c * LANES:(c + 1) * LANES]
    n_rows = kv.shape[0] // CMP_STRIDE
    taps = [[stage_ref[c, pl.ds(m, n_rows, stride=CMP_STRIDE), :] for m in range(CMP_STRIDE)]
            for c in range(n_stage)]
    per_buf = LANES // HEAD_DIM
    for g in range(NSA_GROUPS):
        for out, first in ((kcc_o, 0), (vcc_o, n_stage // 2)):
            c, half = first + g // per_buf, g % per_buf
            out[g] = jnp.concatenate(
                [tap[:, half * HEAD_DIM:(half + 1) * HEAD_DIM] for tap in taps[c]], axis=1).astype(BF16)


def _nsa_prep(proj, batch, seq, tables, sel, sel_t, tr=256):
    n_t = seq // tr
    row = lambda bi, i: bi * n_t + i
    tbl = pl.BlockSpec((tr, LANES), lambda bi, i: (row(bi, i), 0))
    g, h, d = NSA_GROUPS, HEADS_PER_GROUP, HEAD_DIM
    k_o = pl.BlockSpec((None, g, tr, d), lambda bi, i: (bi, 0, i, 0))
    vt_o = pl.BlockSpec((None, g, d + V_AUG_ROWS, tr), lambda bi, i: (bi, 0, 0, i))
    k_sds = jax.ShapeDtypeStruct((batch, g, seq, d), BF16)
    vt_sds = jax.ShapeDtypeStruct((batch, g, d + V_AUG_ROWS, seq), BF16)
    cc_o = pl.BlockSpec((None, g, tr // CMP_STRIDE, CMP_STRIDE * d), lambda bi, i: (bi, 0, i, 0))
    cc_sds = jax.ShapeDtypeStruct((batch, g, seq // CMP_STRIDE, CMP_STRIDE * d), BF16)
    return pl.pallas_call(
        _nsa_prep_kernel,
        grid=(batch, n_t),
        in_specs=[
            pl.BlockSpec((tr, NSA_WIDTH), lambda bi, i: (row(bi, i), COL_Q // NSA_WIDTH)),
            pl.BlockSpec((tr, 6 * NSA_KV_WIDTH), lambda bi, i: (row(bi, i), COL_KV // (6 * NSA_KV_WIDTH))),
            tbl, tbl, tbl,
            pl.BlockSpec(sel.shape, lambda bi, i: (0, 0, 0)),
            pl.BlockSpec(sel_t.shape, lambda bi, i: (0, 0, 0)),
        ],
        out_specs=[pl.BlockSpec((None, g, h, tr, d), lambda bi, i: (bi, 0, 0, i, 0)), k_o, vt_o, k_o, vt_o,
                   cc_o, cc_o],
        out_shape=[jax.ShapeDtypeStruct((batch, g, h, seq, d), BF16), k_sds, vt_sds, k_sds, vt_sds,
                   cc_sds, cc_sds],
        scratch_shapes=[pltpu.VMEM((2 * NSA_KV_WIDTH // LANES, tr, LANES), F32)],
        compiler_params=_params(("parallel", "parallel")),
    )(proj, proj, tables[0], tables[1], tables[2], sel, sel_t)


def _compress_kernel(n_cmp, kc_ref, vc_ref, pek_ref, pev_ref, kw1_ref, kw2_ref, vw1_ref, vw2t_ref,
                     cos_ref, sup_ref, sdn_ref, ko_ref, vto_ref):
    half = kw1_ref.shape[0] // 2
    rows = kc_ref.shape[0]

    def hidden(c_ref, pe_ref, w1_ref):
        c = c_ref[...].astype(BF16)
        pe = jnp.broadcast_to(pe_ref[...], (8, 2 * half)).astype(BF16)
        h = (_dot(c, w1_ref[0:half, :])
             + pltpu.roll(_dot(c, w1_ref[half:2 * half, :]), rows - 1, axis=0)
             + _dot(pe, w1_ref[...])[0:1, :])
        inner = math.sqrt(2.0 / math.pi) * (h + 0.044715 * (h * h * h))
        return (0.5 * h * (1.0 + jnp.tanh(inner))).astype(BF16)

    kc = _rotary(_dot(hidden(kc_ref, pek_ref, kw1_ref), kw2_ref[...]),
                 cos_ref[...], sup_ref[...], sdn_ref[...])
    valid_row = lax.broadcasted_iota(jnp.int32, (rows, 1), 0) < n_cmp
    ko_ref[...] = jnp.where(valid_row, kc[:, 0:HEAD_DIM], 0.0).astype(BF16)
    vct = _dot_nt(vw2t_ref[...], hidden(vc_ref, pev_ref, vw1_ref))
    valid_col = lax.broadcasted_iota(jnp.int32, (1, rows), 1) < n_cmp
    vto_ref[...] = jnp.where(valid_col, vct, 0.0).astype(BF16)


def _compress(kc_in, vc_in, p, cmp_tables, n_cmp):
    bg, rows, width = kc_in.shape
    groups = NSA_GROUPS
    cin = pl.BlockSpec((None, rows, width), lambda i: (i, 0, 0))
    full = lambda shape: pl.BlockSpec(shape, lambda i: (0,) * len(shape))
    tbl = pl.BlockSpec((None, rows, LANES), lambda i: (i // groups, 0, 0))
    return pl.pallas_call(
        functools.partial(_compress_kernel, n_cmp),
        grid=(bg,),
        in_specs=[cin, cin, full((1, 2 * width)), full((1, 2 * width)),
                  full((2 * width, CMP_HIDDEN)), full((CMP_HIDDEN, LANES)),
                  full((2 * width, CMP_HIDDEN)), full((HEAD_DIM, CMP_HIDDEN)),
                  tbl, tbl, tbl],
        out_specs=[pl.BlockSpec((None, rows, HEAD_DIM), lambda i: (i, 0, 0)),
                   pl.BlockSpec((None, HEAD_DIM, rows), lambda i: (i, 0, 0))],
        out_shape=[jax.ShapeDtypeStruct((bg, rows, HEAD_DIM), BF16),
                   jax.ShapeDtypeStruct((bg, HEAD_DIM, rows), BF16)],
        compiler_params=_params(("parallel",)),
    )(kc_in, vc_in, p['pe_k'], p['pe_v'], p['ck_w1'], p['ck_w2'], p['cv_w1'], p['cv_w2t'], *cmp_tables)


def _mask_block_rows(s, allowed, tq):
    heads = s.shape[1] // tq
    return jnp.concatenate(
        [jnp.where(allowed, s[:, h * tq:(h + 1) * tq], MASK_VALUE) for h in range(heads)], axis=1)


def _nsa_attn_kernel(n_cmp, sel_k, q_ref, kc_ref, vct_ref, ks_ref, vst_ref, kw_ref, vwt_ref,
                     gate_ref, zb_ref, ovt_ref, o_ref, thr_ref, gt_ref, sa_ref, sb_ref, ea_ref, eb_ref,
                     sw_ref):
    H = HEADS_PER_GROUP
    D = HEAD_DIM
    tq = q_ref.shape[1]
    nl = H * tq
    n_blk = ovt_ref.shape[0]
    grp = pl.program_id(1)
    t0 = pl.program_id(2) * tq

    qs = q_ref[...].reshape(nl, D)
    tq_pos = t0 + lax.broadcasted_iota(jnp.int32, (1, tq), 1)

    tk = sa_ref.shape[0]
    wlen = sw_ref.shape[0]
    wstart = pl.multiple_of(jnp.maximum(t0 + tq - wlen, 0), tq)

    def scores(kt):
        return _dot_nt(ks_ref[pl.ds(pl.multiple_of(kt * tk, tk), tk), :], qs)

    s_cmp = _dot_nt(kc_ref[...], qs)
    sw_ref[...] = _dot_nt(kw_ref[pl.ds(wstart, wlen), :], qs)
    sa_ref[...] = scores(0)

    n_pad = kc_ref.shape[0]
    crow = lax.broadcasted_iota(jnp.int32, (n_pad, tq), 0)
    cmp_ok = (crow * CMP_STRIDE + (CMP_LEN - 1) <= tq_pos) & (crow < n_cmp)
    s = _mask_block_rows(s_cmp, cmp_ok, tq)
    e = jnp.exp2(s - jnp.max(s, axis=0, keepdims=True))
    has_block = jnp.concatenate([tq_pos >= CMP_LEN - 1] * H, axis=1)
    p = e * jnp.where(has_block, 1.0 / jnp.sum(e, axis=0, keepdims=True), 0.0)
    o_cmp = _dot(vct_ref[...], p.astype(BF16))
    psum = p[:, 0:tq]
    for h in range(1, H):
        psum = psum + p[:, h * tq:(h + 1) * tq]

    p_hi, p_lo = _split2(psum)
    ovt = ovt_ref[...]
    imp = _dot(ovt, p_hi) + _dot(ovt, p_lo)
    blk = lax.broadcasted_iota(jnp.int32, (n_blk, tq), 0)
    tlane = t0 + lax.broadcasted_iota(jnp.int32, (n_blk, tq), 1)
    dist = _head_of(tlane) - blk
    forced = (blk == 0) | ((dist >= 0) & (dist < N_LOCAL_SEL))
    causal_blk = blk * SEL_LEN <= tlane
    imp = jnp.where(causal_blk, imp + jnp.where(forced, FORCE_BONUS, 0.0), MASK_VALUE)
    sub = 8
    row_in_group = lax.broadcasted_iota(jnp.int32, (sub, tq), 0)
    groups = [imp[v * sub:(v + 1) * sub, :] for v in range(n_blk // sub)]
    rank = [jnp.zeros((sub, tq), F32) for _ in groups]
    for j in range(n_blk):
        rj = imp[j:j + 1, :]
        for v, blk_imp in enumerate(groups):
            if v * sub > j:
                ahead = jnp.where(rj >= blk_imp, 1.0, 0.0)
            elif (v + 1) * sub <= j:
                ahead = jnp.where(rj > blk_imp, 1.0, 0.0)
            else:
                ahead = jnp.where(row_in_group > j - v * sub,
                                  jnp.where(rj >= blk_imp, 1.0, 0.0), jnp.where(rj > blk_imp, 1.0, 0.0))
            rank[v] = rank[v] + ahead
    rank = jnp.concatenate(rank, axis=0)
    thr_ref[...] = jnp.where(rank < sel_k, (tlane - blk * SEL_LEN).astype(F32), -1.0)

    blocks_per_tile = tk // SEL_LEN
    n_kt = (t0 + tq + tk - 1) // tk
    last = n_kt - 1
    r_in_blk = lax.broadcasted_iota(jnp.int32, (SEL_LEN, tq), 0).astype(F32)

    def weighted_values(kt, e_buf):
        return _dot(vst_ref[:, pl.ds(pl.multiple_of(kt * tk, tk), tk)], e_buf[...])

    def softmax_step(kt, valid, s_buf, e_buf, m):
        rows = []
        for j in range(blocks_per_tile):
            limit = thr_ref[pl.ds(kt * blocks_per_tile + j, 1), :]
            if valid is not None:
                limit = jnp.where(valid, limit, -1.0)
            rows.append(_mask_block_rows(s_buf[j * SEL_LEN:(j + 1) * SEL_LEN, :], r_in_blk <= limit, tq))
        s = jnp.concatenate(rows, axis=0)
        m_new = jnp.maximum(m, jnp.max(s, axis=0, keepdims=True))
        e_buf[...] = jnp.exp2((s - m_new).astype(BF16))
        return m_new, jnp.exp2(m - m_new)

    eb_ref[...] = jnp.zeros_like(eb_ref)

    def body(i, carry):
        m, acc, alpha_b = carry
        kt_a = 2 * i
        kt_b = kt_a + 1
        pv_b = weighted_values(jnp.maximum(kt_a - 1, 0), eb_ref)
        sb_ref[...] = scores(jnp.minimum(kt_b, last))
        s_a_next = scores(jnp.minimum(kt_a + 2, last))
        m, alpha_a = softmax_step(kt_a, None, sa_ref, ea_ref, m)
        sa_ref[...] = s_a_next
        acc = alpha_b * acc + pv_b
        pv_a = weighted_values(kt_a, ea_ref)
        m, alpha_b = softmax_step(jnp.minimum(kt_b, last), kt_b <= last, sb_ref, eb_ref, m)
        acc = alpha_a * acc + pv_a
        return m, acc, alpha_b

    n_pairs = (n_kt + 1) // 2
    init = (jnp.full((1, nl), MASK_VALUE, F32), jnp.zeros((D + V_AUG_ROWS, nl), F32), jnp.zeros((1, nl), F32))
    _, acc, alpha_b = lax.fori_loop(0, n_pairs, body, init)
    acc = alpha_b * acc + weighted_values(jnp.minimum(2 * n_pairs - 1, last), eb_ref)
    o_slc = acc[0:D] * (1.0 / acc[D:D + 1])

    rel = tq_pos - wstart
    wrow = lax.broadcasted_iota(jnp.int32, (wlen, tq), 0)
    head_rows = (wrow[0:tq] <= rel) & (wrow[0:tq] > rel - WINDOW)
    s = jnp.concatenate([_mask_block_rows(sw_ref[0:tq, :], head_rows, tq),
                         _mask_block_rows(sw_ref[tq:wlen, :], wrow[tq:wlen] <= rel, tq)], axis=0)
    e = jnp.exp2((s - jnp.max(s, axis=0, keepdims=True)).astype(BF16))
    acc = _dot(vwt_ref[:, pl.ds(wstart, wlen)], e)
    o_win = acc[0:D] * (1.0 / acc[D:D + 1])

    gt_ref[...] = jnp.transpose(_sigmoid(gate_ref[...]))

    def gate_row(branch):
        base = (GATE_OFF - LANES) + branch * NSA_HEADS + grp * H
        return jnp.concatenate([gt_ref[pl.ds(base + h, 1), :] for h in range(H)], axis=1)

    mix = gate_row(0) * o_cmp + gate_row(1) * o_slc + gate_row(2) * o_win
    nat = jnp.transpose(jnp.concatenate([mix[:, h * tq:(h + 1) * tq] for h in range(H)], axis=0))
    zb = zb_ref[...]
    o_ref[...] = (nat * (zb * _sigmoid(zb))).astype(BF16)


def _nsa_attn(q_heads, kc, vct, kv, proj, batch, seq, ovt, n_cmp, sel_k, tq=Q_TILE):
    t = batch * seq
    n_q = seq // tq
    assert tq == LANES, "the gate block is transposed as one 128 x 128 tile"
    tk = KEY_TILE if seq >= 2 * KEY_TILE else seq // 2
    wlen = WINDOW + tq if seq >= WINDOW + tq else seq
    ks, vst, kw, vwt = kv
    g, h, d = NSA_GROUPS, HEADS_PER_GROUP, HEAD_DIM
    row = lambda b, gi, i: b * n_q + i
    whole = lambda a: pl.BlockSpec((None, None) + a.shape[2:], lambda b, gi, i: (b, gi, 0, 0))
    kc = kc.reshape((batch, g) + kc.shape[1:])
    vct = vct.reshape((batch, g) + vct.shape[1:])
    return pl.pallas_call(
        functools.partial(_nsa_attn_kernel, n_cmp, sel_k),
        grid=(batch, g, n_q),
        in_specs=[
            pl.BlockSpec((None, None, h, tq, d), lambda b, gi, i: (b, gi, 0, i, 0)),
            whole(kc), whole(vct), whole(ks), whole(vst), whole(kw), whole(vwt),
            pl.BlockSpec((tq, LANES), lambda b, gi, i: (row(b, gi, i), (COL_MISC + LANES) // LANES)),
            pl.BlockSpec((tq, MXU_DIM), lambda b, gi, i: (row(b, gi, i), COL_ZB // MXU_DIM + gi)),
            pl.BlockSpec(ovt.shape, lambda b, gi, i: (0, 0)),
        ],
        out_specs=pl.BlockSpec((tq, MXU_DIM), lambda b, gi, i: (row(b, gi, i), gi)),
        out_shape=jax.ShapeDtypeStruct((t, NSA_WIDTH), BF16),
        scratch_shapes=[pltpu.VMEM((ovt.shape[0], tq), F32), pltpu.VMEM((LANES, tq), F32),
                        pltpu.VMEM((tk, h * tq), F32), pltpu.VMEM((tk, h * tq), F32),
                        pltpu.VMEM((tk, h * tq), BF16), pltpu.VMEM((tk, h * tq), BF16),
                        pltpu.VMEM((wlen, h * tq), F32)],
        compiler_params=_params(("parallel", "parallel", "arbitrary")),
    )(q_heads, kc, vct, ks, vst, kw, vwt, proj, proj, ovt)


def _merge_kernel(ya_ref, yb_ref, wa_ref, wb_ref, ga_ref, gb_ref, o_ref):
    pa = _dot(ya_ref[...], wa_ref[...])
    pb = _dot(yb_ref[...], wb_ref[...])
    o_ref[...] = (_sigmoid(ga_ref[...]) * pa + _sigmoid(gb_ref[...]) * pb).astype(BF16)


def _merge(y_a, y_b, w_a, w_b, proj, layer, tm=512):
    t, c = y_a.shape
    yspec = pl.BlockSpec((tm, c), lambda i: (i, 0))
    wspec = pl.BlockSpec((None, c, D_MODEL), lambda i: (layer, 0, 0))
    return pl.pallas_call(
        _merge_kernel,
        grid=(t // tm,),
        in_specs=[yspec, yspec, wspec, wspec,
                  pl.BlockSpec((tm, D_MODEL), lambda i: (i, COL_MERGE // D_MODEL)),
                  pl.BlockSpec((tm, D_MODEL), lambda i: (i, COL_MERGE // D_MODEL + 1))],
        out_specs=pl.BlockSpec((tm, D_MODEL), lambda i: (i, 0)),
        out_shape=jax.ShapeDtypeStruct((t, D_MODEL), BF16),
        compiler_params=_params(("parallel",)),
    )(y_a, y_b, w_a, w_b, proj, proj)


def _outproj_kernel(m_ref, w_ref, g_ref, x_ref, o_ref):
    out = _dot(m_ref[...], w_ref[...])
    ms = jnp.mean(out * out, axis=-1, keepdims=True)
    o_ref[...] = x_ref[...] + out * lax.rsqrt(ms + NORM_EPS) * g_ref[...]


def _outproj(merged, w_out, gain, x, layer, tm=256):
    t = x.shape[0]
    rows = pl.BlockSpec((tm, D_MODEL), lambda i: (i, 0))
    return pl.pallas_call(
        _outproj_kernel,
        grid=(t // tm,),
        in_specs=[rows, pl.BlockSpec((None, D_MODEL, D_MODEL), lambda i: (layer, 0, 0)),
                  pl.BlockSpec((1, D_MODEL), lambda i: (0, 0)), rows],
        out_specs=rows,
        out_shape=jax.ShapeDtypeStruct((t, D_MODEL), F32),
        compiler_params=_params(("parallel",)),
    )(merged, w_out, gain, x)


def _reorder_w_in(w):
    u_end = SHIFT_WIDTH
    za = u_end
    q = za + RWKV_WIDTH
    kv = q + NSA_WIDTH
    gate = kv + 6 * NSA_KV_WIDTH
    zb = gate + 3 * NSA_HEADS
    merge = zb + NSA_WIDTH
    end = merge + 2 * D_MODEL
    lead = w.shape[:-1]
    pieces = [w[..., merge:end], w[..., 0:3 * RWKV_WIDTH], w[..., za:q], w[..., q:kv], w[..., kv:gate],
              w[..., zb:merge], w[..., 3 * RWKV_WIDTH:u_end], w[..., gate:zb],
              jnp.zeros(lead + (MISC_WIDTH - GATE_OFF - 3 * NSA_HEADS,), w.dtype),
              jnp.zeros(lead + (PROJ_WIDTH - COL_MISC - MISC_WIDTH,), w.dtype)]
    return jnp.concatenate([p.astype(BF16) for p in pieces], axis=-1)


def _rope_tables(pos):
    half = ROT_DIM // 2
    inv_freq = ROPE_THETA ** (-jnp.arange(half, dtype=F32) * (2.0 / ROT_DIM))
    ang = pos.astype(F32)[..., None] * inv_freq
    cos, sin = jnp.cos(ang), jnp.sin(ang)
    d = np.arange(LANES) % HEAD_DIM
    idx = d % half
    cos_l, sin_l = cos[..., idx], sin[..., idx]
    cos_t = jnp.where(d < ROT_DIM, cos_l, 1.0)
    sin_up = jnp.where((d >= half) & (d < ROT_DIM), sin_l, 0.0)
    sin_dn = jnp.where(d < half, -sin_l, 0.0)
    return cos_t, sin_up, sin_dn


def _constants(seq):
    n_cmp = (seq - CMP_LEN) // CMP_STRIDE + 1
    n_sel = seq // SEL_LEN
    n_pad = seq // CMP_STRIDE
    ones_bd = (np.arange(MXU_DIM)[:, None] // HEAD_DIM == np.arange(MXU_DIM)[None, :] // HEAD_DIM)
    sel = np.stack([np.arange(MXU_DIM)[:, None] == i * HEAD_DIM + np.arange(HEAD_DIM)[None, :]
                    for i in range(NSA_GROUPS)])
    ci = np.arange(n_pad)[:, None] * CMP_STRIDE
    sj = np.arange(n_sel)[None, :] * SEL_LEN
    overlap = (ci < sj + SEL_LEN) & (ci + CMP_LEN > sj) & (np.arange(n_pad)[:, None] < n_cmp)
    to_bf = lambda a: jnp.asarray(a.astype(np.float32), BF16)
    return dict(ones_bd=to_bf(ones_bd), sel=to_bf(sel), sel_t=to_bf(sel.transpose(0, 2, 1)),
                overlap_t=to_bf(overlap.T)), n_cmp, n_sel


def _pad_rows(w, rows, at):
    out = jnp.zeros((rows, w.shape[1]), w.dtype)
    return out.at[at:at + w.shape[0]].set(w)


def kernel(x, positions, norm_pre, norm_post, w_in, rwkv_mu, rwkv_w0, rwkv_w_up, rwkv_a0, rwkv_a_up,
           rwkv_k_k, rwkv_k_a, rwkv_r_k, rwkv_gn_w, rwkv_gn_b, rwkv_v0, rwkv_v_down, rwkv_v_up,
           nsa_pe_k, nsa_pe_v, nsa_ck_w1, nsa_ck_w2, nsa_cv_w1, nsa_cv_w2, w_proj_a, w_proj_b, w_out):
    batch, seq, d_model = x.shape
    depth = w_in.shape[0]
    t = batch * seq
    c = RWKV_WIDTH
    consts, n_cmp, n_sel = _constants(seq)
    sel_k = min(SEL_TOPK, n_sel)
    n_pad = seq // CMP_STRIDE

    tables = _rope_tables(positions.reshape(t))
    cmp_pos = positions[:, CMP_LEN - 1::CMP_STRIDE]
    cmp_pos = jnp.concatenate([cmp_pos, cmp_pos[:, -1:]], axis=1)
    cmp_tables = _rope_tables(cmp_pos)

    row = lambda a: a.reshape(1, -1).astype(F32)
    xf = x.reshape(t, d_model)
    v_first = None
    w_in_b = _reorder_w_in(w_in)
    w_a_b, w_b_b, w_out_b = w_proj_a.astype(BF16), w_proj_b.astype(BF16), w_out.astype(BF16)
    for l in range(depth):
        proj = _inproj(xf, row(norm_pre[l]), w_in_b, l)

        mu = rwkv_mu[l]
        rp = dict(
            mu_r=row(mu[0:c]), mu_k=row(mu[c:2 * c]), mu_v=row(mu[2 * c:3 * c]),
            mu_m=row(jnp.concatenate([mu[3 * c:], jnp.zeros((MISC_WIDTH - GATE_OFF,), F32)])),
            w0=row(rwkv_w0[l]), a0=row(rwkv_a0[l]), k_k=row(rwkv_k_k[l]), k_a=row(rwkv_k_a[l]),
            r_k=row(rwkv_r_k[l]),
            w_up=_pad_rows(rwkv_w_up[l], MISC_WIDTH, 0).astype(BF16),
            a_up=_pad_rows(rwkv_a_up[l], MISC_WIDTH, DECAY_LORA).astype(BF16),
            ones_bd=consts['ones_bd'], gn_w=row(rwkv_gn_w[l]), gn_b=row(rwkv_gn_b[l]))
        vres = None
        if l > 0:
            vres = dict(v0=row(rwkv_v0[l - 1]),
                        v_dn=jnp.pad(rwkv_v_down[l - 1], ((0, 0), (0, LANES - VRES_LORA))).astype(BF16),
                        v_up=_pad_rows(rwkv_v_up[l - 1], LANES, 0).astype(BF16))
        y_a, vf = _wkv(proj, batch, seq, rp, vres, v_first)
        if l == 0:
            v_first = vf

        q_heads, ks, vst, kw, vwt, kcc, vcc = _nsa_prep(proj, batch, seq, tables, consts['sel'],
                                                        consts['sel_t'])
        cmp_rows = lambda a: a.reshape(batch * NSA_GROUPS, n_pad, CMP_STRIDE * HEAD_DIM)
        cp = dict(pe_k=nsa_pe_k[l].reshape(1, -1), pe_v=nsa_pe_v[l].reshape(1, -1),
                  ck_w1=nsa_ck_w1[l].astype(BF16),
                  ck_w2=jnp.tile(nsa_ck_w2[l], (1, LANES // HEAD_DIM)).astype(BF16),
                  cv_w1=nsa_cv_w1[l].astype(BF16), cv_w2t=nsa_cv_w2[l].T.astype(BF16))
        kc, vct = _compress(cmp_rows(kcc), cmp_rows(vcc), cp, cmp_tables, n_cmp)
        y_b = _nsa_attn(q_heads, kc, vct, (ks, vst, kw, vwt), proj, batch, seq, consts['overlap_t'],
                        n_cmp, sel_k)

        merged = _merge(y_a, y_b, w_a_b, w_b_b, proj, l)
        xf = _outproj(merged, w_out_b, row(norm_post[l]), xf, l)
    return xf.reshape(batch, seq, d_model)
```

```python
import functools
import math

import numpy as np
import jax
import jax.numpy as jnp
from jax import lax
from jax.experimental import pallas as pl
from jax.experimental.pallas import tpu as pltpu

F32 = jnp.float32
BF16 = jnp.bfloat16

D_MODEL = 2048
RWKV_HEADS = 16
HEAD_DIM = 64
RWKV_WIDTH = RWKV_HEADS * HEAD_DIM
DECAY_LORA = 96
ICLR_LORA = 96
VRES_LORA = 64
GN_EPS = 64e-5
L2_EPS = 1e-12
SHIFT_WIDTH = 3 * RWKV_WIDTH + DECAY_LORA + ICLR_LORA

NSA_HEADS = 16
NSA_GROUPS = 4
HEADS_PER_GROUP = NSA_HEADS // NSA_GROUPS
NSA_WIDTH = NSA_HEADS * HEAD_DIM
NSA_KV_WIDTH = NSA_GROUPS * HEAD_DIM
CMP_LEN = 32
CMP_STRIDE = 16
CMP_HIDDEN = 256
SEL_LEN = 64
SEL_TOPK = 16
N_LOCAL_SEL = 2
WINDOW = 512
ROPE_THETA = 500000.0
ROT_DIM = HEAD_DIM // 4
FORCE_BONUS = 1e3
MASK_VALUE = -1e30
NORM_EPS = 1e-6

LANES = 128
MXU_DIM = 256
VMEM_LIMIT = 56 * 1024 * 1024

COL_MERGE = 0
COL_R = 4096
COL_K = 5120
COL_V = 6144
COL_ZA = 7168
COL_Q = 8192
COL_KV = 9216
COL_ZB = 10752
COL_MISC = 11776
MISC_WIDTH = 256
GATE_OFF = DECAY_LORA + ICLR_LORA
PROJ_WIDTH = 12288

WKV_CHUNK = 64
WKV_BATCH = 2
HEAD_GROUP_LANES = 256
Q_TILE = 128
KEY_TILE = 256
V_AUG_ROWS = 16
LOG2E = 1.4426950408889634


def _dot(a, b):
    return jnp.dot(a, b, preferred_element_type=F32)


def _dot_nt(a, b):
    return lax.dot_general(a, b, (((1,), (1,)), ((), ())), preferred_element_type=F32)


def _dot_tn(a, b):
    return lax.dot_general(a, b, (((0,), (0,)), ((), ())), preferred_element_type=F32)


def _split2(x):
    hi = x.astype(BF16)
    lo = (x - hi.astype(F32)).astype(BF16)
    return hi, lo


def _dot2(x, w):
    hi, lo = _split2(x)
    return _dot(hi, w) + _dot(lo, w)


def _head_sum(x, ones_bd):
    width = ones_bd.shape[0]
    parts = []
    for c in range(x.shape[1] // width):
        parts.append(_dot2(x[:, c * width:(c + 1) * width], ones_bd))
    return jnp.concatenate(parts, axis=1)


def _sigmoid(x):
    return 1.0 / (1.0 + jnp.exp(-x))


def _head_of(idx):
    return jnp.right_shift(idx, int(math.log2(HEAD_DIM)))


def _params(sem):
    return pltpu.CompilerParams(dimension_semantics=sem, vmem_limit_bytes=VMEM_LIMIT)


def _inproj_kernel(x_ref, g_ref, w_ref, o_ref, h_ref):
    @pl.when(pl.program_id(1) == 0)
    def _():
        x = x_ref[...]
        ms = jnp.mean(x * x, axis=-1, keepdims=True)
        h_ref[...] = (x * lax.rsqrt(ms + NORM_EPS) * g_ref[...]).astype(BF16)

    o_ref[...] = _dot(h_ref[...], w_ref[...])


def _inproj(x, gain, w, layer, tm=1024, tn=1024):
    t = x.shape[0]
    return pl.pallas_call(
        _inproj_kernel,
        grid=(t // tm, PROJ_WIDTH // tn),
        in_specs=[
            pl.BlockSpec((tm, D_MODEL), lambda i, j: (i, 0)),
            pl.BlockSpec((1, D_MODEL), lambda i, j: (0, 0)),
            pl.BlockSpec((None, D_MODEL, tn), lambda i, j: (layer, 0, j)),
        ],
        out_specs=pl.BlockSpec((tm, tn), lambda i, j: (i, j)),
        out_shape=jax.ShapeDtypeStruct((t, PROJ_WIDTH), F32),
        scratch_shapes=[pltpu.VMEM((tm, D_MODEL), BF16)],
        compiler_params=_params(("parallel", "arbitrary")),
    )(x, gain, w)


def _wkv_kernel(has_vres, *refs):
    (r_ref, k_ref, v_ref, m_ref, za_ref, mu_r, mu_k, mu_v, mu_m, w0_ref, a0_ref, kk_w, ka_w, rk_w,
     wup_ref, aup_ref, gnw_ref, gnb_ref, ones_ref) = refs[:19]
    pos = 19
    if has_vres:
        v0_ref, vdn_ref, vup_ref, vf_ref = refs[pos:pos + 4]
        pos += 4
        o_ref = refs[pos]
        pos += 1
    else:
        o_ref, vf_out = refs[pos:pos + 2]
        pos += 2
    st_ref, lr_ref, lk_ref, lv_ref, lm_ref = refs[pos:pos + 5]

    nb, L, width = r_ref.shape
    W = HEAD_GROUP_LANES
    n_groups = width // W
    first_chunk = pl.program_id(1) == 0

    @pl.when(first_chunk)
    def _():
        st_ref[...] = jnp.zeros_like(st_ref)

    rows = nb * L
    ti = lax.broadcasted_iota(jnp.int32, (rows, rows), 0)
    tj = lax.broadcasted_iota(jnp.int32, (rows, rows), 1)
    tri = jnp.where((_head_of(ti) == _head_of(tj)) & (ti >= tj), 1.0, 0.0).astype(BF16)
    ri = lax.broadcasted_iota(jnp.int32, (W, W), 0)
    ci = lax.broadcasted_iota(jnp.int32, (W, W), 1)
    to_bf = lambda idx: idx.astype(F32).astype(BF16)
    same_head = to_bf(_head_of(ri)) == to_bf(_head_of(ci))
    row_id = lax.broadcasted_iota(jnp.int32, (rows, 1), 0)
    ones_bd = ones_ref[...]
    tail = lr_ref.shape[1]

    def stack4(a):
        return jnp.concatenate([a] * 4, axis=0)

    def xform(a):
        return jnp.where(same_head, stack4(a.astype(BF16)), 0.0)

    def shifted(u_ref, last_ref, mu_ref):
        u = u_ref[...].reshape(rows, u_ref.shape[2])
        prev = pltpu.roll(u, 1, axis=0)
        for bi in range(nb):
            last = last_ref[bi, tail - 1:tail, :]
            last = jnp.where(first_chunk, jnp.zeros_like(last), last)
            prev = jnp.where(row_id == bi * L, last, prev)
            last_ref[bi] = u[(bi + 1) * L - tail:(bi + 1) * L, :]
        return u + (prev - u) * mu_ref[...]

    r = shifted(r_ref, lr_ref, mu_r)
    k = shifted(k_ref, lk_ref, mu_k)
    v = shifted(v_ref, lv_ref, mu_v)
    misc = shifted(m_ref, lm_ref, mu_m)
    z = -(w0_ref[...] + _dot(jnp.tanh(misc).astype(BF16), wup_ref[...]))
    softplus = jnp.maximum(z, 0.0) + jnp.log(1.0 + jnp.exp(-jnp.abs(z)))
    lw = -jnp.exp(-softplus - 0.5)
    a = _sigmoid(a0_ref[...] + _dot(misc.astype(BF16), aup_ref[...]))
    if has_vres:
        low = _dot(v.astype(BF16), vdn_ref[...])
        gate = _sigmoid(v0_ref[...] + _dot(low.astype(BF16), vup_ref[...]))
        v = v + (vf_ref[...].reshape(rows, width) - v) * gate
    else:
        vf_out[...] = v.reshape(nb, L, width)
    kk = k * kk_w[...]
    kk = kk / jnp.maximum(jnp.sqrt(_head_sum(kk * kk, ones_bd)), L2_EPS)
    k = k * (1.0 + (a - 1.0) * ka_w[...])
    b = kk * a
    bonus = _head_sum(r * k * rk_w[...], ones_bd) * v

    lw_hi, lw_lo = _split2(lw)
    cum = _dot(tri, lw_hi) + _dot(tri, lw_lo)
    ab_all = -(jnp.exp(cum - lw) * kk)
    rb_all = jnp.exp(cum) * r

    abx, rbc, vx, kpx, bpx, bk, pcol, ar = [], [], [], [], [], [], [], []
    for bi in range(nb):
        rs = slice(bi * L, (bi + 1) * L)
        cum_b = cum[rs]
        pend = cum_b[L - 1:L, :]
        e_n = jnp.exp(-cum_b)
        e_e = jnp.exp(pend - cum_b)
        bb_all = e_n * b[rs]
        kb_all = e_n * k[rs]
        bp_all = e_e * b[rs]
        kp_all = e_e * k[rs]
        p_end = jnp.exp(pend)
        for g in range(n_groups):
            sl = slice(g * W, (g + 1) * W)
            abx.append(xform(ab_all[rs, sl]))
            rbc.append(rb_all[rs, sl].astype(BF16))
            vx.append(xform(v[rs, sl]))
            kpx.append(xform(kp_all[:, sl]))
            bpx.append(xform(bp_all[:, sl]))
            bk.append(jnp.concatenate([xform(bb_all[:, sl]), xform(kb_all[:, sl])], axis=0))
            ar.append(jnp.concatenate([ab_all[rs, sl], rb_all[rs, sl]], axis=0).astype(BF16))
            pcol.append(jnp.transpose(jnp.broadcast_to(p_end[:, sl], (W, W))))
    chains = range(nb * n_groups)

    tc = lax.broadcasted_iota(jnp.int32, (L, W), 0)
    sc = jnp.bitwise_and(lax.broadcasted_iota(jnp.int32, (L, W), 1), HEAD_DIM - 1)
    g = [_dot_nt(ar[c], bk[c]) for c in chains]
    nm = [jnp.where(tc > sc, g[c][0:L, 0:W], 0.0) for c in chains]
    akrk = [jnp.concatenate([jnp.where(tc > sc, g[c][0:L, W:2 * W], 0.0),
                             jnp.where(tc >= sc, g[c][L:2 * L, W:2 * W], 0.0)], axis=0).astype(BF16)
            for c in chains]
    rbm = [jnp.where(tc >= sc, g[c][L:2 * L, 0:W], 0.0).astype(BF16) for c in chains]

    tinv = [jnp.where(tc == sc, 1.0, 0.0) + nm[c] for c in chains]
    pw = [nm[c].astype(BF16) for c in chains]
    for _ in range(5):
        pw = [_dot(pw[c], xform(pw[c])).astype(BF16) for c in chains]
        tinv = [tinv[c] + _dot(pw[c], xform(tinv[c])) for c in chains]
    tb = [tinv[c].astype(BF16) for c in chains]

    akrk_v = [_dot(akrk[c], vx[c]) for c in chains]
    kv = [_dot_tn(kpx[c], vx[c]) for c in chains]
    wu = [_dot(tb[c], jnp.concatenate([abx[c], xform(akrk_v[c][0:L])], axis=1)) for c in chains]

    st = [st_ref[c] for c in chains]
    stb = [st[c].astype(BF16) for c in chains]
    ub = [xform(_dot(wu[c][:, 0:W].astype(BF16), stb[c]) + wu[c][:, W:2 * W]) for c in chains]
    for c in chains:
        st_ref[c] = pcol[c] * st[c] + _dot_tn(bpx[c], ub[c]) + kv[c]
    yc = [_dot(jnp.concatenate([rbc[c], rbm[c]], axis=1), jnp.concatenate([stb[c], ub[c]], axis=0))
          + akrk_v[c][L:2 * L] for c in chains]

    inv_n = 1.0 / HEAD_DIM
    per_elem = []
    for bi in range(nb):
        per_elem.append(jnp.concatenate([yc[bi * n_groups + gi] for gi in range(n_groups)], axis=1))
    y = jnp.concatenate(per_elem, axis=0)
    mean = _head_sum(y, ones_bd) * inv_n
    d = y - mean
    var = _head_sum(d * d, ones_bd) * inv_n
    yn = d * lax.rsqrt(var + GN_EPS) * gnw_ref[...] + gnb_ref[...]
    za = za_ref[...].reshape(rows, width)
    o_ref[...] = ((yn + bonus) * (za * _sigmoid(za))).astype(BF16).reshape(nb, L, width)


def _wkv(proj, batch, seq, p, vres, v_first, nb=WKV_BATCH):
    c = RWKV_WIDTH
    nb = min(nb, batch)
    n_chunks = seq // WKV_CHUNK
    has_vres = vres is not None
    proj3 = proj.reshape(batch, seq, proj.shape[1])

    def cols(width, col):
        return pl.BlockSpec((nb, WKV_CHUNK, width), lambda bi, ci, _c=col // width: (bi, ci, _c))

    def full(shape):
        return pl.BlockSpec(shape, lambda bi, ci: (0,) * len(shape))

    vec = full((1, c))
    in_specs = [cols(c, COL_R), cols(c, COL_K), cols(c, COL_V), cols(MISC_WIDTH, COL_MISC), cols(c, COL_ZA),
                vec, vec, vec, full((1, MISC_WIDTH)), vec, vec, vec, vec, vec,
                full((MISC_WIDTH, c)), full((MISC_WIDTH, c)), vec, vec,
                full((HEAD_GROUP_LANES, HEAD_GROUP_LANES))]
    args = [proj3, proj3, proj3, proj3, proj3, p['mu_r'], p['mu_k'], p['mu_v'], p['mu_m'], p['w0'], p['a0'],
            p['k_k'], p['k_a'], p['r_k'], p['w_up'], p['a_up'], p['gn_w'], p['gn_b'], p['ones_bd']]
    rows = pl.BlockSpec((nb, WKV_CHUNK, c), lambda bi, ci: (bi, ci, 0))
    y_sds = jax.ShapeDtypeStruct((batch, seq, c), BF16)
    if has_vres:
        in_specs += [vec, full((c, LANES)), full((LANES, c)), rows]
        args += [vres['v0'], vres['v_dn'], vres['v_up'], v_first]
        out_specs, out_shape = rows, y_sds
    else:
        out_specs, out_shape = [rows, rows], [y_sds, jax.ShapeDtypeStruct((batch, seq, c), F32)]
    tail = 8
    out = pl.pallas_call(
        functools.partial(_wkv_kernel, has_vres),
        grid=(batch // nb, n_chunks),
        in_specs=in_specs,
        out_specs=out_specs,
        out_shape=out_shape,
        scratch_shapes=[pltpu.VMEM((nb * c // HEAD_GROUP_LANES, HEAD_GROUP_LANES, HEAD_GROUP_LANES), F32),
                        pltpu.VMEM((nb, tail, c), F32), pltpu.VMEM((nb, tail, c), F32),
                        pltpu.VMEM((nb, tail, c), F32), pltpu.VMEM((nb, tail, MISC_WIDTH), F32)],
        compiler_params=_params(("parallel", "arbitrary")),
    )(*args)
    if has_vres:
        return out.reshape(batch * seq, c), None
    return out[0].reshape(batch * seq, c), out[1]


def _rotary(x, cos_t, sin_up, sin_dn):
    reps = x.shape[1] // LANES
    half = ROT_DIM // 2
    tile = lambda tbl: jnp.concatenate([tbl] * reps, axis=1) if reps > 1 else tbl
    return (x * tile(cos_t) + pltpu.roll(x, half, axis=1) * tile(sin_up)
            + pltpu.roll(x, x.shape[1] - half, axis=1) * tile(sin_dn))


def _nsa_prep_kernel(q_ref, kv_ref, cos_ref, sup_ref, sdn_ref, sel_ref, selt_ref,
                     q_o, ks_o, vst_o, kw_o, vwt_o, kcc_o, vcc_o, stage_ref):
    cos_t, sin_up, sin_dn = cos_ref[...], sup_ref[...], sdn_ref[...]
    q = (_rotary(q_ref[...], cos_t, sin_up, sin_dn) * (HEAD_DIM ** -0.5 * LOG2E)).astype(BF16)
    w = NSA_KV_WIDTH
    kv = kv_ref[...]
    ks = _rotary(kv[:, 2 * w:3 * w], cos_t, sin_up, sin_dn).astype(BF16)
    vs = kv[:, 3 * w:4 * w].astype(BF16)
    kw = _rotary(kv[:, 4 * w:5 * w], cos_t, sin_up, sin_dn).astype(BF16)
    vw = kv[:, 5 * w:6 * w].astype(BF16)
    ones = jnp.ones((V_AUG_ROWS, q.shape[0]), BF16)
    for g in range(NSA_GROUPS):
        qg = q[:, g * MXU_DIM:(g + 1) * MXU_DIM]
        for h in range(HEADS_PER_GROUP):
            q_o[g, h] = _dot(qg, sel_ref[h]).astype(BF16)
        ks_o[g] = _dot(ks, sel_ref[g]).astype(BF16)
        kw_o[g] = _dot(kw, sel_ref[g]).astype(BF16)
        vst_o[g] = jnp.concatenate([_dot_nt(selt_ref[g], vs).astype(BF16), ones], axis=0)
        vwt_o[g] = jnp.concatenate([_dot_nt(selt_ref[g], vw).astype(BF16), ones], axis=0)
    n_stage = stage_ref.shape[0]
    for c in range(n_stage):
        stage_ref[c] = kv[:, c * LANES:(c + 1) * LANES]
    n_rows = kv.shape[0] // CMP_STRIDE
    taps = [[stage_ref[c, pl.ds(m, n_rows, stride=CMP_STRIDE), :] for m in range(CMP_STRIDE)]
            for c in range(n_stage)]
    per_buf = LANES // HEAD_DIM
    for g in range(NSA_GROUPS):
        for out, first in ((kcc_o, 0), (vcc_o, n_stage // 2)):
            c, half = first + g // per_buf, g % per_buf
            out[g] = jnp.concatenate(
                [tap[:, half * HEAD_DIM:(half + 1) * HEAD_DIM] for tap in taps[c]], axis=1).astype(BF16)


def _nsa_prep(proj, batch, seq, tables, sel, sel_t, tr=256):
    n_t = seq // tr
    row = lambda bi, i: bi * n_t + i
    tbl = pl.BlockSpec((tr, LANES), lambda bi, i: (row(bi, i), 0))
    g, h, d = NSA_GROUPS, HEADS_PER_GROUP, HEAD_DIM
    k_o = pl.BlockSpec((None, g, tr, d), lambda bi, i: (bi, 0, i, 0))
    vt_o = pl.BlockSpec((None, g, d + V_AUG_ROWS, tr), lambda bi, i: (bi, 0, 0, i))
    k_sds = jax.ShapeDtypeStruct((batch, g, seq, d), BF16)
    vt_sds = jax.ShapeDtypeStruct((batch, g, d + V_AUG_ROWS, seq), BF16)
    cc_o = pl.BlockSpec((None, g, tr // CMP_STRIDE, CMP_STRIDE * d), lambda bi, i: (bi, 0, i, 0))
    cc_sds = jax.ShapeDtypeStruct((batch, g, seq // CMP_STRIDE, CMP_STRIDE * d), BF16)
    return pl.pallas_call(
        _nsa_prep_kernel,
        grid=(batch, n_t),
        in_specs=[
            pl.BlockSpec((tr, NSA_WIDTH), lambda bi, i: (row(bi, i), COL_Q // NSA_WIDTH)),
            pl.BlockSpec((tr, 6 * NSA_KV_WIDTH), lambda bi, i: (row(bi, i), COL_KV // (6 * NSA_KV_WIDTH))),
            tbl, tbl, tbl,
            pl.BlockSpec(sel.shape, lambda bi, i: (0, 0, 0)),
            pl.BlockSpec(sel_t.shape, lambda bi, i: (0, 0, 0)),
        ],
        out_specs=[pl.BlockSpec((None, g, h, tr, d), lambda bi, i: (bi, 0, 0, i, 0)), k_o, vt_o, k_o, vt_o,
                   cc_o, cc_o],
        out_shape=[jax.ShapeDtypeStruct((batch, g, h, seq, d), BF16), k_sds, vt_sds, k_sds, vt_sds,
                   cc_sds, cc_sds],
        scratch_shapes=[pltpu.VMEM((2 * NSA_KV_WIDTH // LANES, tr, LANES), F32)],
        compiler_params=_params(("parallel", "parallel")),
    )(proj, proj, tables[0], tables[1], tables[2], sel, sel_t)


def _compress_kernel(n_cmp, kc_ref, vc_ref, pek_ref, pev_ref, kw1_ref, kw2_ref, vw1_ref, vw2t_ref,
                     cos_ref, sup_ref, sdn_ref, ko_ref, vto_ref):
    half = kw1_ref.shape[0] // 2
    rows = kc_ref.shape[0]

    def hidden(c_ref, pe_ref, w1_ref):
        c = c_ref[...].astype(BF16)
        pe = jnp.broadcast_to(pe_ref[...], (8, 2 * half)).astype(BF16)
        h = (_dot(c, w1_ref[0:half, :])
             + pltpu.roll(_dot(c, w1_ref[half:2 * half, :]), rows - 1, axis=0)
             + _dot(pe, w1_ref[...])[0:1, :])
        inner = math.sqrt(2.0 / math.pi) * (h + 0.044715 * (h * h * h))
        return (0.5 * h * (1.0 + jnp.tanh(inner))).astype(BF16)

    kc = _rotary(_dot(hidden(kc_ref, pek_ref, kw1_ref), kw2_ref[...]),
                 cos_ref[...], sup_ref[...], sdn_ref[...])
    valid_row = lax.broadcasted_iota(jnp.int32, (rows, 1), 0) < n_cmp
    ko_ref[...] = jnp.where(valid_row, kc[:, 0:HEAD_DIM], 0.0).astype(BF16)
    vct = _dot_nt(vw2t_ref[...], hidden(vc_ref, pev_ref, vw1_ref))
    valid_col = lax.broadcasted_iota(jnp.int32, (1, rows), 1) < n_cmp
    vto_ref[...] = jnp.where(valid_col, vct, 0.0).astype(BF16)


def _compress(kc_in, vc_in, p, cmp_tables, n_cmp):
    bg, rows, width = kc_in.shape
    groups = NSA_GROUPS
    cin = pl.BlockSpec((None, rows, width), lambda i: (i, 0, 0))
    full = lambda shape: pl.BlockSpec(shape, lambda i: (0,) * len(shape))
    tbl = pl.BlockSpec((None, rows, LANES), lambda i: (i // groups, 0, 0))
    return pl.pallas_call(
        functools.partial(_compress_kernel, n_cmp),
        grid=(bg,),
        in_specs=[cin, cin, full((1, 2 * width)), full((1, 2 * width)),
                  full((2 * width, CMP_HIDDEN)), full((CMP_HIDDEN, LANES)),
                  full((2 * width, CMP_HIDDEN)), full((HEAD_DIM, CMP_HIDDEN)),
                  tbl, tbl, tbl],
        out_specs=[pl.BlockSpec((None, rows, HEAD_DIM), lambda i: (i, 0, 0)),
                   pl.BlockSpec((None, HEAD_DIM, rows), lambda i: (i, 0, 0))],
        out_shape=[jax.ShapeDtypeStruct((bg, rows, HEAD_DIM), BF16),
                   jax.ShapeDtypeStruct((bg, HEAD_DIM, rows), BF16)],
        compiler_params=_params(("parallel",)),
    )(kc_in, vc_in, p['pe_k'], p['pe_v'], p['ck_w1'], p['ck_w2'], p['cv_w1'], p['cv_w2t'], *cmp_tables)


def _mask_block_rows(s, allowed, tq):
    heads = s.shape[1] // tq
    return jnp.concatenate(
        [jnp.where(allowed, s[:, h * tq:(h + 1) * tq], MASK_VALUE) for h in range(heads)], axis=1)


def _nsa_attn_kernel(n_cmp, sel_k, q_ref, kc_ref, vct_ref, ks_ref, vst_ref, kw_ref, vwt_ref,
                     gate_ref, zb_ref, ovt_ref, o_ref, thr_ref, gt_ref, sa_ref, sb_ref, ea_ref, eb_ref,
                     sw_ref):
    H = HEADS_PER_GROUP
    D = HEAD_DIM
    tq = q_ref.shape[1]
    nl = H * tq
    n_blk = ovt_ref.shape[0]
    grp = pl.program_id(1)
    t0 = pl.program_id(2) * tq

    qs = q_ref[...].reshape(nl, D)
    tq_pos = t0 + lax.broadcasted_iota(jnp.int32, (1, tq), 1)

    tk = sa_ref.shape[0]
    wlen = sw_ref.shape[0]
    wstart = pl.multiple_of(jnp.maximum(t0 + tq - wlen, 0), tq)

    def scores(kt):
        return _dot_nt(ks_ref[pl.ds(pl.multiple_of(kt * tk, tk), tk), :], qs)

    s_cmp = _dot_nt(kc_ref[...], qs)
    sw_ref[...] = _dot_nt(kw_ref[pl.ds(wstart, wlen), :], qs)
    sa_ref[...] = scores(0).astype(BF16)

    n_pad = kc_ref.shape[0]
    crow = lax.broadcasted_iota(jnp.int32, (n_pad, tq), 0)
    cmp_ok = (crow * CMP_STRIDE + (CMP_LEN - 1) <= tq_pos) & (crow < n_cmp)
    s = _mask_block_rows(s_cmp, cmp_ok, tq)
    e = jnp.exp2(s - jnp.max(s, axis=0, keepdims=True))
    has_block = jnp.concatenate([tq_pos >= CMP_LEN - 1] * H, axis=1)
    p = e * jnp.where(has_block, 1.0 / jnp.sum(e, axis=0, keepdims=True), 0.0)
    o_cmp = _dot(vct_ref[...], p.astype(BF16))
    psum = p[:, 0:tq]
    for h in range(1, H):
        psum = psum + p[:, h * tq:(h + 1) * tq]

    p_hi, p_lo = _split2(psum)
    ovt = ovt_ref[...]
    imp = _dot(ovt, p_hi) + _dot(ovt, p_lo)
    blk = lax.broadcasted_iota(jnp.int32, (n_blk, tq), 0)
    tlane = t0 + lax.broadcasted_iota(jnp.int32, (n_blk, tq), 1)
    dist = _head_of(tlane) - blk
    forced = (blk == 0) | ((dist >= 0) & (dist < N_LOCAL_SEL))
    causal_blk = blk * SEL_LEN <= tlane
    imp = jnp.where(causal_blk, imp + jnp.where(forced, FORCE_BONUS, 0.0), MASK_VALUE)
    sub = 8
    row_in_group = lax.broadcasted_iota(jnp.int32, (sub, tq), 0)
    groups = [imp[v * sub:(v + 1) * sub, :] for v in range(n_blk // sub)]
    rank = [jnp.zeros((sub, tq), F32) for _ in groups]
    for j in range(n_blk):
        rj = imp[j:j + 1, :]
        for v, blk_imp in enumerate(groups):
            if v * sub > j:
                ahead = jnp.where(rj >= blk_imp, 1.0, 0.0)
            elif (v + 1) * sub <= j:
                ahead = jnp.where(rj > blk_imp, 1.0, 0.0)
            else:
                ahead = jnp.where(row_in_group > j - v * sub,
                                  jnp.where(rj >= blk_imp, 1.0, 0.0), jnp.where(rj > blk_imp, 1.0, 0.0))
            rank[v] = rank[v] + ahead
    rank = jnp.concatenate(rank, axis=0)
    thr_ref[...] = jnp.where(rank < sel_k, (tlane - blk * SEL_LEN).astype(F32), -1.0)

    blocks_per_tile = tk // SEL_LEN
    n_kt = (t0 + tq + tk - 1) // tk
    last = n_kt - 1
    r_in_blk = lax.broadcasted_iota(jnp.int32, (SEL_LEN, tq), 0).astype(F32).astype(BF16)

    def weighted_values(kt, e_buf):
        return _dot(vst_ref[:, pl.ds(pl.multiple_of(kt * tk, tk), tk)], e_buf[...])

    def softmax_step(kt, valid, s_buf, e_buf, m):
        rows = []
        for j in range(blocks_per_tile):
            limit = thr_ref[pl.ds(kt * blocks_per_tile + j, 1), :]
            if valid is not None:
                limit = jnp.where(valid, limit, -1.0)
            limit = jnp.minimum(limit, float(SEL_LEN)).astype(BF16)
            rows.append(_mask_block_rows(s_buf[j * SEL_LEN:(j + 1) * SEL_LEN, :], r_in_blk <= limit, tq))
        s = jnp.concatenate(rows, axis=0)
        m_new = jnp.maximum(m, jnp.max(s, axis=0, keepdims=True).astype(F32))
        e_buf[...] = jnp.exp2(s - m_new.astype(BF16))
        return m_new, jnp.exp2(m - m_new)

    eb_ref[...] = jnp.zeros_like(eb_ref)

    def body(i, carry):
        m, acc, alpha_b = carry
        kt_a = 2 * i
        kt_b = kt_a + 1
        pv_b = weighted_values(jnp.maximum(kt_a - 1, 0), eb_ref)
        sb_ref[...] = scores(jnp.minimum(kt_b, last)).astype(BF16)
        s_a_next = scores(jnp.minimum(kt_a + 2, last)).astype(BF16)
        m, alpha_a = softmax_step(kt_a, None, sa_ref, ea_ref, m)
        sa_ref[...] = s_a_next
        acc = alpha_b * acc + pv_b
        pv_a = weighted_values(kt_a, ea_ref)
        m, alpha_b = softmax_step(jnp.minimum(kt_b, last), kt_b <= last, sb_ref, eb_ref, m)
        acc = alpha_a * acc + pv_a
        return m, acc, alpha_b

    n_pairs = (n_kt + 1) // 2
    init = (jnp.full((1, nl), MASK_VALUE, F32), jnp.zeros((D + V_AUG_ROWS, nl), F32), jnp.zeros((1, nl), F32))
    _, acc, alpha_b = lax.fori_loop(0, n_pairs, body, init)
    acc = alpha_b * acc + weighted_values(jnp.minimum(2 * n_pairs - 1, last), eb_ref)
    o_slc = acc[0:D] * (1.0 / acc[D:D + 1])

    rel = tq_pos - wstart
    wrow = lax.broadcasted_iota(jnp.int32, (wlen, tq), 0)
    head_rows = (wrow[0:tq] <= rel) & (wrow[0:tq] > rel - WINDOW)
    s = jnp.concatenate([_mask_block_rows(sw_ref[0:tq, :], head_rows, tq),
                         _mask_block_rows(sw_ref[tq:wlen, :], wrow[tq:wlen] <= rel, tq)], axis=0)
    e = jnp.exp2((s - jnp.max(s, axis=0, keepdims=True)).astype(BF16))
    acc = _dot(vwt_ref[:, pl.ds(wstart, wlen)], e)
    o_win = acc[0:D] * (1.0 / acc[D:D + 1])

    gt_ref[...] = jnp.transpose(_sigmoid(gate_ref[...]))

    def gate_row(branch):
        base = (GATE_OFF - LANES) + branch * NSA_HEADS + grp * H
        return jnp.concatenate([gt_ref[pl.ds(base + h, 1), :] for h in range(H)], axis=1)

    mix = gate_row(0) * o_cmp + gate_row(1) * o_slc + gate_row(2) * o_win
    nat = jnp.transpose(jnp.concatenate([mix[:, h * tq:(h + 1) * tq] for h in range(H)], axis=0))
    zb = zb_ref[...]
    o_ref[...] = (nat * (zb * _sigmoid(zb))).astype(BF16)


def _nsa_attn(q_heads, kc, vct, kv, proj, batch, seq, ovt, n_cmp, sel_k, tq=Q_TILE):
    t = batch * seq
    n_q = seq // tq
    assert tq == LANES, "the gate block is transposed as one 128 x 128 tile"
    tk = KEY_TILE if seq >= 2 * KEY_TILE else seq // 2
    wlen = WINDOW + tq if seq >= WINDOW + tq else seq
    ks, vst, kw, vwt = kv
    g, h, d = NSA_GROUPS, HEADS_PER_GROUP, HEAD_DIM
    row = lambda b, gi, i: b * n_q + i
    whole = lambda a: pl.BlockSpec((None, None) + a.shape[2:], lambda b, gi, i: (b, gi, 0, 0))
    kc = kc.reshape((batch, g) + kc.shape[1:])
    vct = vct.reshape((batch, g) + vct.shape[1:])
    return pl.pallas_call(
        functools.partial(_nsa_attn_kernel, n_cmp, sel_k),
        grid=(batch, g, n_q),
        in_specs=[
            pl.BlockSpec((None, None, h, tq, d), lambda b, gi, i: (b, gi, 0, i, 0)),
            whole(kc), whole(vct), whole(ks), whole(vst), whole(kw), whole(vwt),
            pl.BlockSpec((tq, LANES), lambda b, gi, i: (row(b, gi, i), (COL_MISC + LANES) // LANES)),
            pl.BlockSpec((tq, MXU_DIM), lambda b, gi, i: (row(b, gi, i), COL_ZB // MXU_DIM + gi)),
            pl.BlockSpec(ovt.shape, lambda b, gi, i: (0, 0)),
        ],
        out_specs=pl.BlockSpec((tq, MXU_DIM), lambda b, gi, i: (row(b, gi, i), gi)),
        out_shape=jax.ShapeDtypeStruct((t, NSA_WIDTH), BF16),
        scratch_shapes=[pltpu.VMEM((ovt.shape[0], tq), F32), pltpu.VMEM((LANES, tq), F32),
                        pltpu.VMEM((tk, h * tq), BF16), pltpu.VMEM((tk, h * tq), BF16),
                        pltpu.VMEM((tk, h * tq), BF16), pltpu.VMEM((tk, h * tq), BF16),
                        pltpu.VMEM((wlen, h * tq), F32)],
        compiler_params=_params(("parallel", "parallel", "arbitrary")),
    )(q_heads, kc, vct, ks, vst, kw, vwt, proj, proj, ovt)


def _merge_kernel(ya_ref, yb_ref, wa_ref, wb_ref, ga_ref, gb_ref, o_ref):
    pa = _dot(ya_ref[...], wa_ref[...])
    pb = _dot(yb_ref[...], wb_ref[...])
    o_ref[...] = (_sigmoid(ga_ref[...]) * pa + _sigmoid(gb_ref[...]) * pb).astype(BF16)


def _merge(y_a, y_b, w_a, w_b, proj, layer, tm=512):
    t, c = y_a.shape
    yspec = pl.BlockSpec((tm, c), lambda i: (i, 0))
    wspec = pl.BlockSpec((None, c, D_MODEL), lambda i: (layer, 0, 0))
    return pl.pallas_call(
        _merge_kernel,
        grid=(t // tm,),
        in_specs=[yspec, yspec, wspec, wspec,
                  pl.BlockSpec((tm, D_MODEL), lambda i: (i, COL_MERGE // D_MODEL)),
                  pl.BlockSpec((tm, D_MODEL), lambda i: (i, COL_MERGE // D_MODEL + 1))],
        out_specs=pl.BlockSpec((tm, D_MODEL), lambda i: (i, 0)),
        out_shape=jax.ShapeDtypeStruct((t, D_MODEL), BF16),
        compiler_params=_params(("parallel",)),
    )(y_a, y_b, w_a, w_b, proj, proj)


def _outproj_kernel(m_ref, w_ref, g_ref, x_ref, o_ref):
    out = _dot(m_ref[...], w_ref[...])
    ms = jnp.mean(out * out, axis=-1, keepdims=True)
    o_ref[...] = x_ref[...] + out * lax.rsqrt(ms + NORM_EPS) * g_ref[...]


def _outproj(merged, w_out, gain, x, layer, tm=256):
    t = x.shape[0]
    rows = pl.BlockSpec((tm, D_MODEL), lambda i: (i, 0))
    return pl.pallas_call(
        _outproj_kernel,
        grid=(t // tm,),
        in_specs=[rows, pl.BlockSpec((None, D_MODEL, D_MODEL), lambda i: (layer, 0, 0)),
                  pl.BlockSpec((1, D_MODEL), lambda i: (0, 0)), rows],
        out_specs=rows,
        out_shape=jax.ShapeDtypeStruct((t, D_MODEL), F32),
        compiler_params=_params(("parallel",)),
    )(merged, w_out, gain, x)


def _reorder_w_in(w):
    u_end = SHIFT_WIDTH
    za = u_end
    q = za + RWKV_WIDTH
    kv = q + NSA_WIDTH
    gate = kv + 6 * NSA_KV_WIDTH
    zb = gate + 3 * NSA_HEADS
    merge = zb + NSA_WIDTH
    end = merge + 2 * D_MODEL
    lead = w.shape[:-1]
    pieces = [w[..., merge:end], w[..., 0:3 * RWKV_WIDTH], w[..., za:q], w[..., q:kv], w[..., kv:gate],
              w[..., zb:merge], w[..., 3 * RWKV_WIDTH:u_end], w[..., gate:zb],
              jnp.zeros(lead + (MISC_WIDTH - GATE_OFF - 3 * NSA_HEADS,), w.dtype),
              jnp.zeros(lead + (PROJ_WIDTH - COL_MISC - MISC_WIDTH,), w.dtype)]
    return jnp.concatenate([p.astype(BF16) for p in pieces], axis=-1)


def _rope_tables(pos):
    half = ROT_DIM // 2
    inv_freq = ROPE_THETA ** (-jnp.arange(half, dtype=F32) * (2.0 / ROT_DIM))
    ang = pos.astype(F32)[..., None] * inv_freq
    cos, sin = jnp.cos(ang), jnp.sin(ang)
    d = np.arange(LANES) % HEAD_DIM
    idx = d % half
    cos_l, sin_l = cos[..., idx], sin[..., idx]
    cos_t = jnp.where(d < ROT_DIM, cos_l, 1.0)
    sin_up = jnp.where((d >= half) & (d < ROT_DIM), sin_l, 0.0)
    sin_dn = jnp.where(d < half, -sin_l, 0.0)
    return cos_t, sin_up, sin_dn


def _constants(seq):
    n_cmp = (seq - CMP_LEN) // CMP_STRIDE + 1
    n_sel = seq // SEL_LEN
    n_pad = seq // CMP_STRIDE
    ones_bd = (np.arange(MXU_DIM)[:, None] // HEAD_DIM == np.arange(MXU_DIM)[None, :] // HEAD_DIM)
    sel = np.stack([np.arange(MXU_DIM)[:, None] == i * HEAD_DIM + np.arange(HEAD_DIM)[None, :]
                    for i in range(NSA_GROUPS)])
    ci = np.arange(n_pad)[:, None] * CMP_STRIDE
    sj = np.arange(n_sel)[None, :] * SEL_LEN
    overlap = (ci < sj + SEL_LEN) & (ci + CMP_LEN > sj) & (np.arange(n_pad)[:, None] < n_cmp)
    to_bf = lambda a: jnp.asarray(a.astype(np.float32), BF16)
    return dict(ones_bd=to_bf(ones_bd), sel=to_bf(sel), sel_t=to_bf(sel.transpose(0, 2, 1)),
                overlap_t=to_bf(overlap.T)), n_cmp, n_sel


def _pad_rows(w, rows, at):
    out = jnp.zeros((rows, w.shape[1]), w.dtype)
    return out.at[at:at + w.shape[0]].set(w)


def kernel(x, positions, norm_pre, norm_post, w_in, rwkv_mu, rwkv_w0, rwkv_w_up, rwkv_a0, rwkv_a_up,
           rwkv_k_k, rwkv_k_a, rwkv_r_k, rwkv_gn_w, rwkv_gn_b, rwkv_v0, rwkv_v_down, rwkv_v_up,
           nsa_pe_k, nsa_pe_v, nsa_ck_w1, nsa_ck_w2, nsa_cv_w1, nsa_cv_w2, w_proj_a, w_proj_b, w_out):
    batch, seq, d_model = x.shape
    depth = w_in.shape[0]
    t = batch * seq
    c = RWKV_WIDTH
    consts, n_cmp, n_sel = _constants(seq)
    sel_k = min(SEL_TOPK, n_sel)
    n_pad = seq // CMP_STRIDE

    tables = _rope_tables(positions.reshape(t))
    cmp_pos = positions[:, CMP_LEN - 1::CMP_STRIDE]
    cmp_pos = jnp.concatenate([cmp_pos, cmp_pos[:, -1:]], axis=1)
    cmp_tables = _rope_tables(cmp_pos)

    row = lambda a: a.reshape(1, -1).astype(F32)
    xf = x.reshape(t, d_model)
    v_first = None
    w_in_b = _reorder_w_in(w_in)
    w_a_b, w_b_b, w_out_b = w_proj_a.astype(BF16), w_proj_b.astype(BF16), w_out.astype(BF16)
    for l in range(depth):
        proj = _inproj(xf, row(norm_pre[l]), w_in_b, l)

        mu = rwkv_mu[l]
        rp = dict(
            mu_r=row(mu[0:c]), mu_k=row(mu[c:2 * c]), mu_v=row(mu[2 * c:3 * c]),
            mu_m=row(jnp.concatenate([mu[3 * c:], jnp.zeros((MISC_WIDTH - GATE_OFF,), F32)])),
            w0=row(rwkv_w0[l]), a0=row(rwkv_a0[l]), k_k=row(rwkv_k_k[l]), k_a=row(rwkv_k_a[l]),
            r_k=row(rwkv_r_k[l]),
            w_up=_pad_rows(rwkv_w_up[l], MISC_WIDTH, 0).astype(BF16),
            a_up=_pad_rows(rwkv_a_up[l], MISC_WIDTH, DECAY_LORA).astype(BF16),
            ones_bd=consts['ones_bd'], gn_w=row(rwkv_gn_w[l]), gn_b=row(rwkv_gn_b[l]))
        vres = None
        if l > 0:
            vres = dict(v0=row(rwkv_v0[l - 1]),
                        v_dn=jnp.pad(rwkv_v_down[l - 1], ((0, 0), (0, LANES - VRES_LORA))).astype(BF16),
                        v_up=_pad_rows(rwkv_v_up[l - 1], LANES, 0).astype(BF16))
        y_a, vf = _wkv(proj, batch, seq, rp, vres, v_first)
        if l == 0:
            v_first = vf

        q_heads, ks, vst, kw, vwt, kcc, vcc = _nsa_prep(proj, batch, seq, tables, consts['sel'],
                                                        consts['sel_t'])
        cmp_rows = lambda a: a.reshape(batch * NSA_GROUPS, n_pad, CMP_STRIDE * HEAD_DIM)
        cp = dict(pe_k=nsa_pe_k[l].reshape(1, -1), pe_v=nsa_pe_v[l].reshape(1, -1),
                  ck_w1=nsa_ck_w1[l].astype(BF16),
                  ck_w2=jnp.tile(nsa_ck_w2[l], (1, LANES // HEAD_DIM)).astype(BF16),
                  cv_w1=nsa_cv_w1[l].astype(BF16), cv_w2t=nsa_cv_w2[l].T.astype(BF16))
        kc, vct = _compress(cmp_rows(kcc), cmp_rows(vcc), cp, cmp_tables, n_cmp)
        y_b = _nsa_attn(q_heads, kc, vct, (ks, vst, kw, vwt), proj, batch, seq, consts['overlap_t'],
                        n_cmp, sel_k)

        merged = _merge(y_a, y_b, w_a_b, w_b_b, proj, l)
        xf = _outproj(merged, w_out_b, row(norm_post[l]), xf, l)
    return xf.reshape(batch, seq, d_model)
```

```python
import functools
import math

import numpy as np
import jax
import jax.numpy as jnp
from jax import lax
from jax.experimental import pallas as pl
from jax.experimental.pallas import tpu as pltpu

F32 = jnp.float32
BF16 = jnp.bfloat16

D_MODEL = 2048
RWKV_HEADS = 16
HEAD_DIM = 64
RWKV_WIDTH = RWKV_HEADS * HEAD_DIM
DECAY_LORA = 96
ICLR_LORA = 96
VRES_LORA = 64
GN_EPS = 64e-5
L2_EPS = 1e-12
SHIFT_WIDTH = 3 * RWKV_WIDTH + DECAY_LORA + ICLR_LORA

NSA_HEADS = 16
NSA_GROUPS = 4
HEADS_PER_GROUP = NSA_HEADS // NSA_GROUPS
NSA_WIDTH = NSA_HEADS * HEAD_DIM
NSA_KV_WIDTH = NSA_GROUPS * HEAD_DIM
CMP_LEN = 32
CMP_STRIDE = 16
CMP_HIDDEN = 256
SEL_LEN = 64
SEL_TOPK = 16
N_LOCAL_SEL = 2
WINDOW = 512
ROPE_THETA = 500000.0
ROT_DIM = HEAD_DIM // 4
FORCE_BONUS = 1e3
MASK_VALUE = -1e30
NORM_EPS = 1e-6

LANES = 128
MXU_DIM = 256
VMEM_LIMIT = 56 * 1024 * 1024

COL_MERGE = 0
COL_R = 4096
COL_K = 5120
COL_V = 6144
COL_ZA = 7168
COL_Q = 8192
COL_KV = 9216
COL_ZB = 10752
COL_MISC = 11776
MISC_WIDTH = 256
GATE_OFF = DECAY_LORA + ICLR_LORA
PROJ_WIDTH = 12288

WKV_CHUNK = 64
WKV_BATCH = 2
HEAD_GROUP_LANES = 256
Q_TILE = 128
KEY_TILE = 256
V_AUG_ROWS = 16
LOG2E = 1.4426950408889634


def _dot(a, b):
    return jnp.dot(a, b, preferred_element_type=F32)


def _dot_nt(a, b):
    return lax.dot_general(a, b, (((1,), (1,)), ((), ())), preferred_element_type=F32)


def _dot_tn(a, b):
    return lax.dot_general(a, b, (((0,), (0,)), ((), ())), preferred_element_type=F32)


def _split2(x):
    hi = x.astype(BF16)
    lo = (x - hi.astype(F32)).astype(BF16)
    return hi, lo


def _dot2(x, w):
    hi, lo = _split2(x)
    return _dot(hi, w) + _dot(lo, w)


def _head_sum(x, ones_bd):
    width = ones_bd.shape[0]
    parts = []
    for c in range(x.shape[1] // width):
        parts.append(_dot2(x[:, c * width:(c + 1) * width], ones_bd))
    return jnp.concatenate(parts, axis=1)


def _sigmoid(x):
    return 1.0 / (1.0 + jnp.exp(-x))


def _head_of(idx):
    return jnp.right_shift(idx, int(math.log2(HEAD_DIM)))


def _params(sem):
    return pltpu.CompilerParams(dimension_semantics=sem, vmem_limit_bytes=VMEM_LIMIT)


def _inproj_kernel(x_ref, g_ref, w_ref, o_ref, h_ref):
    @pl.when(pl.program_id(1) == 0)
    def _():
        x = x_ref[...]
        ms = jnp.mean(x * x, axis=-1, keepdims=True)
        h_ref[...] = (x * lax.rsqrt(ms + NORM_EPS) * g_ref[...]).astype(BF16)

    o_ref[...] = _dot(h_ref[...], w_ref[...])


def _inproj(x, gain, w, layer, tm=1024, tn=1024):
    t = x.shape[0]
    return pl.pallas_call(
        _inproj_kernel,
        grid=(t // tm, PROJ_WIDTH // tn),
        in_specs=[
            pl.BlockSpec((tm, D_MODEL), lambda i, j: (i, 0)),
            pl.BlockSpec((1, D_MODEL), lambda i, j: (0, 0)),
            pl.BlockSpec((None, D_MODEL, tn), lambda i, j: (layer, 0, j)),
        ],
        out_specs=pl.BlockSpec((tm, tn), lambda i, j: (i, j)),
        out_shape=jax.ShapeDtypeStruct((t, PROJ_WIDTH), F32),
        scratch_shapes=[pltpu.VMEM((tm, D_MODEL), BF16)],
        compiler_params=_params(("parallel", "arbitrary")),
    )(x, gain, w)


def _wkv_kernel(has_vres, *refs):
    (r_ref, k_ref, v_ref, m_ref, za_ref, mu_r, mu_k, mu_v, mu_m, w0_ref, a0_ref, kk_w, ka_w, rk_w,
     wup_ref, aup_ref, gnw_ref, gnb_ref, ones_ref) = refs[:19]
    pos = 19
    if has_vres:
        v0_ref, vdn_ref, vup_ref, vf_ref = refs[pos:pos + 4]
        pos += 4
        o_ref = refs[pos]
        pos += 1
    else:
        o_ref, vf_out = refs[pos:pos + 2]
        pos += 2
    st_ref, lr_ref, lk_ref, lv_ref, lm_ref = refs[pos:pos + 5]

    nb, L, width = r_ref.shape
    W = HEAD_GROUP_LANES
    n_groups = width // W
    first_chunk = pl.program_id(1) == 0

    @pl.when(first_chunk)
    def _():
        st_ref[...] = jnp.zeros_like(st_ref)

    rows = nb * L
    ti = lax.broadcasted_iota(jnp.int32, (rows, rows), 0)
    tj = lax.broadcasted_iota(jnp.int32, (rows, rows), 1)
    tri = jnp.where((_head_of(ti) == _head_of(tj)) & (ti >= tj), 1.0, 0.0).astype(BF16)
    ri = lax.broadcasted_iota(jnp.int32, (W, W), 0)
    ci = lax.broadcasted_iota(jnp.int32, (W, W), 1)
    to_bf = lambda idx: idx.astype(F32).astype(BF16)
    same_head = to_bf(_head_of(ri)) == to_bf(_head_of(ci))
    row_id = lax.broadcasted_iota(jnp.int32, (rows, 1), 0)
    ones_bd = ones_ref[...]
    tail = lr_ref.shape[1]

    def stack4(a):
        return jnp.concatenate([a] * 4, axis=0)

    def xform(a):
        return jnp.where(same_head, stack4(a.astype(BF16)), 0.0)

    def shifted(u_ref, last_ref, mu_ref):
        u = u_ref[...].reshape(rows, u_ref.shape[2])
        prev = pltpu.roll(u, 1, axis=0)
        for bi in range(nb):
            last = last_ref[bi, tail - 1:tail, :]
            last = jnp.where(first_chunk, jnp.zeros_like(last), last)
            prev = jnp.where(row_id == bi * L, last, prev)
            last_ref[bi] = u[(bi + 1) * L - tail:(bi + 1) * L, :]
        return u + (prev - u) * mu_ref[...]

    r = shifted(r_ref, lr_ref, mu_r)
    k = shifted(k_ref, lk_ref, mu_k)
    v = shifted(v_ref, lv_ref, mu_v)
    misc = shifted(m_ref, lm_ref, mu_m)
    z = -(w0_ref[...] + _dot(jnp.tanh(misc).astype(BF16), wup_ref[...]))
    softplus = jnp.maximum(z, 0.0) + jnp.log(1.0 + jnp.exp(-jnp.abs(z)))
    lw = -jnp.exp(-softplus - 0.5)
    a = _sigmoid(a0_ref[...] + _dot(misc.astype(BF16), aup_ref[...]))
    if has_vres:
        low = _dot(v.astype(BF16), vdn_ref[...])
        gate = _sigmoid(v0_ref[...] + _dot(low.astype(BF16), vup_ref[...]))
        v = v + (vf_ref[...].reshape(rows, width) - v) * gate
    else:
        vf_out[...] = v.reshape(nb, L, width)
    kk = k * kk_w[...]
    kk = kk / jnp.maximum(jnp.sqrt(_head_sum(kk * kk, ones_bd)), L2_EPS)
    k = k * (1.0 + (a - 1.0) * ka_w[...])
    b = kk * a
    bonus = _head_sum(r * k * rk_w[...], ones_bd) * v

    lw_hi, lw_lo = _split2(lw)
    cum = _dot(tri, lw_hi) + _dot(tri, lw_lo)
    ab_all = -(jnp.exp(cum - lw) * kk)
    rb_all = jnp.exp(cum) * r

    abx, rbc, vx, kpx, bpx, bk, pcol, ar = [], [], [], [], [], [], [], []
    for bi in range(nb):
        rs = slice(bi * L, (bi + 1) * L)
        cum_b = cum[rs]
        pend = cum_b[L - 1:L, :]
        e_n = jnp.exp(-cum_b)
        e_e = jnp.exp(pend - cum_b)
        bb_all = e_n * b[rs]
        kb_all = e_n * k[rs]
        bp_all = e_e * b[rs]
        kp_all = e_e * k[rs]
        p_end = jnp.exp(pend)
        for g in range(n_groups):
            sl = slice(g * W, (g + 1) * W)
            abx.append(xform(ab_all[rs, sl]))
            rbc.append(rb_all[rs, sl].astype(BF16))
            vx.append(xform(v[rs, sl]))
            kpx.append(xform(kp_all[:, sl]))
            bpx.append(xform(bp_all[:, sl]))
            bk.append(jnp.concatenate([xform(bb_all[:, sl]), xform(kb_all[:, sl])], axis=0))
            ar.append(jnp.concatenate([ab_all[rs, sl], rb_all[rs, sl]], axis=0).astype(BF16))
            pcol.append(jnp.transpose(jnp.broadcast_to(p_end[:, sl], (W, W))))
    chains = range(nb * n_groups)

    tc = lax.broadcasted_iota(jnp.int32, (L, W), 0)
    sc = jnp.bitwise_and(lax.broadcasted_iota(jnp.int32, (L, W), 1), HEAD_DIM - 1)
    g = [_dot_nt(ar[c], bk[c]) for c in chains]
    nm = [jnp.where(tc > sc, g[c][0:L, 0:W], 0.0) for c in chains]
    akrk = [jnp.concatenate([jnp.where(tc > sc, g[c][0:L, W:2 * W], 0.0),
                             jnp.where(tc >= sc, g[c][L:2 * L, W:2 * W], 0.0)], axis=0).astype(BF16)
            for c in chains]
    rbm = [jnp.where(tc >= sc, g[c][L:2 * L, 0:W], 0.0).astype(BF16) for c in chains]

    tinv = [jnp.where(tc == sc, 1.0, 0.0) + nm[c] for c in chains]
    pw = [nm[c].astype(BF16) for c in chains]
    for _ in range(5):
        pw = [_dot(pw[c], xform(pw[c])).astype(BF16) for c in chains]
        tinv = [tinv[c] + _dot(pw[c], xform(tinv[c])) for c in chains]
    tb = [tinv[c].astype(BF16) for c in chains]

    akrk_v = [_dot(akrk[c], vx[c]) for c in chains]
    kv = [_dot_tn(kpx[c], vx[c]) for c in chains]
    wu = [_dot(tb[c], jnp.concatenate([abx[c], xform(akrk_v[c][0:L])], axis=1)) for c in chains]

    st = [st_ref[c] for c in chains]
    stb = [st[c].astype(BF16) for c in chains]
    ub = [xform(_dot(wu[c][:, 0:W].astype(BF16), stb[c]) + wu[c][:, W:2 * W]) for c in chains]
    for c in chains:
        st_ref[c] = pcol[c] * st[c] + _dot_tn(bpx[c], ub[c]) + kv[c]
    yc = [_dot(jnp.concatenate([rbc[c], rbm[c]], axis=1), jnp.concatenate([stb[c], ub[c]], axis=0))
          + akrk_v[c][L:2 * L] for c in chains]

    inv_n = 1.0 / HEAD_DIM
    per_elem = []
    for bi in range(nb):
        per_elem.append(jnp.concatenate([yc[bi * n_groups + gi] for gi in range(n_groups)], axis=1))
    y = jnp.concatenate(per_elem, axis=0)
    mean = _head_sum(y, ones_bd) * inv_n
    d = y - mean
    var = _head_sum(d * d, ones_bd) * inv_n
    yn = d * lax.rsqrt(var + GN_EPS) * gnw_ref[...] + gnb_ref[...]
    za = za_ref[...].reshape(rows, width)
    o_ref[...] = ((yn + bonus) * (za * _sigmoid(za))).astype(BF16).reshape(nb, L, width)


def _wkv(proj, batch, seq, p, vres, v_first, nb=WKV_BATCH):
    c = RWKV_WIDTH
    nb = min(nb, batch)
    n_chunks = seq // WKV_CHUNK
    has_vres = vres is not None
    proj3 = proj.reshape(batch, seq, proj.shape[1])

    def cols(width, col):
        return pl.BlockSpec((nb, WKV_CHUNK, width), lambda bi, ci, _c=col // width: (bi, ci, _c))

    def full(shape):
        return pl.BlockSpec(shape, lambda bi, ci: (0,) * len(shape))

    vec = full((1, c))
    in_specs = [cols(c, COL_R), cols(c, COL_K), cols(c, COL_V), cols(MISC_WIDTH, COL_MISC), cols(c, COL_ZA),
                vec, vec, vec, full((1, MISC_WIDTH)), vec, vec, vec, vec, vec,
                full((MISC_WIDTH, c)), full((MISC_WIDTH, c)), vec, vec,
                full((HEAD_GROUP_LANES, HEAD_GROUP_LANES))]
    args = [proj3, proj3, proj3, proj3, proj3, p['mu_r'], p['mu_k'], p['mu_v'], p['mu_m'], p['w0'], p['a0'],
            p['k_k'], p['k_a'], p['r_k'], p['w_up'], p['a_up'], p['gn_w'], p['gn_b'], p['ones_bd']]
    rows = pl.BlockSpec((nb, WKV_CHUNK, c), lambda bi, ci: (bi, ci, 0))
    y_sds = jax.ShapeDtypeStruct((batch, seq, c), BF16)
    if has_vres:
        in_specs += [vec, full((c, LANES)), full((LANES, c)), rows]
        args += [vres['v0'], vres['v_dn'], vres['v_up'], v_first]
        out_specs, out_shape = rows, y_sds
    else:
        out_specs, out_shape = [rows, rows], [y_sds, jax.ShapeDtypeStruct((batch, seq, c), F32)]
    tail = 8
    out = pl.pallas_call(
        functools.partial(_wkv_kernel, has_vres),
        grid=(batch // nb, n_chunks),
        in_specs=in_specs,
        out_specs=out_specs,
        out_shape=out_shape,
        scratch_shapes=[pltpu.VMEM((nb * c // HEAD_GROUP_LANES, HEAD_GROUP_LANES, HEAD_GROUP_LANES), F32),
                        pltpu.VMEM((nb, tail, c), F32), pltpu.VMEM((nb, tail, c), F32),
                        pltpu.VMEM((nb, tail, c), F32), pltpu.VMEM((nb, tail, MISC_WIDTH), F32)],
        compiler_params=_params(("parallel", "arbitrary")),
    )(*args)
    if has_vres:
        return out.reshape(batch * seq, c), None
    return out[0].reshape(batch * seq, c), out[1]


def _rotary(x, cos_t, sin_up, sin_dn):
    reps = x.shape[1] // LANES
    half = ROT_DIM // 2
    tile = lambda tbl: jnp.concatenate([tbl] * reps, axis=1) if reps > 1 else tbl
    return (x * tile(cos_t) + pltpu.roll(x, half, axis=1) * tile(sin_up)
            + pltpu.roll(x, x.shape[1] - half, axis=1) * tile(sin_dn))


def _nsa_prep_kernel(q_ref, kv_ref, cos_ref, sup_ref, sdn_ref, sel_ref, selt_ref,
                     q_o, ks_o, vst_o, kw_o, vwt_o, kcc_o, vcc_o, stage_ref):
    cos_t, sin_up, sin_dn = cos_ref[...], sup_ref[...], sdn_ref[...]
    q = (_rotary(q_ref[...], cos_t, sin_up, sin_dn) * (HEAD_DIM ** -0.5 * LOG2E)).astype(BF16)
    w = NSA_KV_WIDTH
    kv = kv_ref[...]
    ks = _rotary(kv[:, 2 * w:3 * w], cos_t, sin_up, sin_dn).astype(BF16)
    vs = kv[:, 3 * w:4 * w].astype(BF16)
    kw = _rotary(kv[:, 4 * w:5 * w], cos_t, sin_up, sin_dn).astype(BF16)
    vw = kv[:, 5 * w:6 * w].astype(BF16)
    ones = jnp.ones((V_AUG_ROWS, q.shape[0]), BF16)
    for g in range(NSA_GROUPS):
        qg = q[:, g * MXU_DIM:(g + 1) * MXU_DIM]
        for h in range(HEADS_PER_GROUP):
            q_o[g, h] = _dot(qg, sel_ref[h]).astype(BF16)
        ks_o[g] = _dot(ks, sel_ref[g]).astype(BF16)
        kw_o[g] = _dot(kw, sel_ref[g]).astype(BF16)
        vst_o[g] = jnp.concatenate([_dot_nt(selt_ref[g], vs).astype(BF16), ones], axis=0)
        vwt_o[g] = jnp.concatenate([_dot_nt(selt_ref[g], vw).astype(BF16), ones], axis=0)
    n_stage = stage_ref.shape[0]
    for c in range(n_stage):
        stage_ref[c] = kv[:, c * LANES:(c + 1) * LANES]
    n_rows = kv.shape[0] // CMP_STRIDE
    taps = [[stage_ref[c, pl.ds(m, n_rows, stride=CMP_STRIDE), :] for m in range(CMP_STRIDE)]
            for c in range(n_stage)]
    per_buf = LANES // HEAD_DIM
    for g in range(NSA_GROUPS):
        for out, first in ((kcc_o, 0), (vcc_o, n_stage // 2)):
            c, half = first + g // per_buf, g % per_buf
            out[g] = jnp.concatenate(
                [tap[:, half * HEAD_DIM:(half + 1) * HEAD_DIM] for tap in taps[c]], axis=1).astype(BF16)


def _nsa_prep(proj, batch, seq, tables, sel, sel_t, tr=256):
    n_t = seq // tr
    row = lambda bi, i: bi * n_t + i
    tbl = pl.BlockSpec((tr, LANES), lambda bi, i: (row(bi, i), 0))
    g, h, d = NSA_GROUPS, HEADS_PER_GROUP, HEAD_DIM
    k_o = pl.BlockSpec((None, g, tr, d), lambda bi, i: (bi, 0, i, 0))
    vt_o = pl.BlockSpec((None, g, d + V_AUG_ROWS, tr), lambda bi, i: (bi, 0, 0, i))
    k_sds = jax.ShapeDtypeStruct((batch, g, seq, d), BF16)
    vt_sds = jax.ShapeDtypeStruct((batch, g, d + V_AUG_ROWS, seq), BF16)
    cc_o = pl.BlockSpec((None, g, tr // CMP_STRIDE, CMP_STRIDE * d), lambda bi, i: (bi, 0, i, 0))
    cc_sds = jax.ShapeDtypeStruct((batch, g, seq // CMP_STRIDE, CMP_STRIDE * d), BF16)
    return pl.pallas_call(
        _nsa_prep_kernel,
        grid=(batch, n_t),
        in_specs=[
            pl.BlockSpec((tr, NSA_WIDTH), lambda bi, i: (row(bi, i), COL_Q // NSA_WIDTH)),
            pl.BlockSpec((tr, 6 * NSA_KV_WIDTH), lambda bi, i: (row(bi, i), COL_KV // (6 * NSA_KV_WIDTH))),
            tbl, tbl, tbl,
            pl.BlockSpec(sel.shape, lambda bi, i: (0, 0, 0)),
            pl.BlockSpec(sel_t.shape, lambda bi, i: (0, 0, 0)),
        ],
        out_specs=[pl.BlockSpec((None, g, h, tr, d), lambda bi, i: (bi, 0, 0, i, 0)), k_o, vt_o, k_o, vt_o,
                   cc_o, cc_o],
        out_shape=[jax.ShapeDtypeStruct((batch, g, h, seq, d), BF16), k_sds, vt_sds, k_sds, vt_sds,
                   cc_sds, cc_sds],
        scratch_shapes=[pltpu.VMEM((2 * NSA_KV_WIDTH // LANES, tr, LANES), F32)],
        compiler_params=_params(("parallel", "parallel")),
    )(proj, proj, tables[0], tables[1], tables[2], sel, sel_t)


def _compress_kernel(n_cmp, kc_ref, vc_ref, pek_ref, pev_ref, kw1_ref, kw2_ref, vw1_ref, vw2t_ref,
                     cos_ref, sup_ref, sdn_ref, ko_ref, vto_ref):
    half = kw1_ref.shape[0] // 2
    rows = kc_ref.shape[0]

    def hidden(c_ref, pe_ref, w1_ref):
        c = c_ref[...].astype(BF16)
        pe = jnp.broadcast_to(pe_ref[...], (8, 2 * half)).astype(BF16)
        h = (_dot(c, w1_ref[0:half, :])
             + pltpu.roll(_dot(c, w1_ref[half:2 * half, :]), rows - 1, axis=0)
             + _dot(pe, w1_ref[...])[0:1, :])
        inner = math.sqrt(2.0 / math.pi) * (h + 0.044715 * (h * h * h))
        return (0.5 * h * (1.0 + jnp.tanh(inner))).astype(BF16)

    kc = _rotary(_dot(hidden(kc_ref, pek_ref, kw1_ref), kw2_ref[...]),
                 cos_ref[...], sup_ref[...], sdn_ref[...])
    valid_row = lax.broadcasted_iota(jnp.int32, (rows, 1), 0) < n_cmp
    ko_ref[...] = jnp.where(valid_row, kc[:, 0:HEAD_DIM], 0.0).astype(BF16)
    vct = _dot_nt(vw2t_ref[...], hidden(vc_ref, pev_ref, vw1_ref))
    valid_col = lax.broadcasted_iota(jnp.int32, (1, rows), 1) < n_cmp
    vto_ref[...] = jnp.where(valid_col, vct, 0.0).astype(BF16)


def _compress(kc_in, vc_in, p, cmp_tables, n_cmp):
    bg, rows, width = kc_in.shape
    groups = NSA_GROUPS
    cin = pl.BlockSpec((None, rows, width), lambda i: (i, 0, 0))
    full = lambda shape: pl.BlockSpec(shape, lambda i: (0,) * len(shape))
    tbl = pl.BlockSpec((None, rows, LANES), lambda i: (i // groups, 0, 0))
    return pl.pallas_call(
        functools.partial(_compress_kernel, n_cmp),
        grid=(bg,),
        in_specs=[cin, cin, full((1, 2 * width)), full((1, 2 * width)),
                  full((2 * width, CMP_HIDDEN)), full((CMP_HIDDEN, LANES)),
                  full((2 * width, CMP_HIDDEN)), full((HEAD_DIM, CMP_HIDDEN)),
                  tbl, tbl, tbl],
        out_specs=[pl.BlockSpec((None, rows, HEAD_DIM), lambda i: (i, 0, 0)),
                   pl.BlockSpec((None, HEAD_DIM, rows), lambda i: (i, 0, 0))],
        out_shape=[jax.ShapeDtypeStruct((bg, rows, HEAD_DIM), BF16),
                   jax.ShapeDtypeStruct((bg, HEAD_DIM, rows), BF16)],
        compiler_params=_params(("parallel",)),
    )(kc_in, vc_in, p['pe_k'], p['pe_v'], p['ck_w1'], p['ck_w2'], p['cv_w1'], p['cv_w2t'], *cmp_tables)


def _mask_block_rows(s, allowed, tq):
    heads = s.shape[1] // tq
    return jnp.concatenate(
        [jnp.where(allowed, s[:, h * tq:(h + 1) * tq], MASK_VALUE) for h in range(heads)], axis=1)


def _nsa_attn_kernel(n_cmp, sel_k, q_ref, kc_ref, vct_ref, ks_ref, vst_ref, kw_ref, vwt_ref,
                     gate_ref, zb_ref, ovt_ref, o_ref, thr_ref, gt_ref, sa_ref, sb_ref, ea_ref, eb_ref,
                     sw_ref):
    H = HEADS_PER_GROUP
    D = HEAD_DIM
    tq = q_ref.shape[1]
    nl = H * tq
    n_blk = ovt_ref.shape[0]
    grp = pl.program_id(1)
    t0 = pl.program_id(2) * tq

    qs = q_ref[...].reshape(nl, D)
    tq_pos = t0 + lax.broadcasted_iota(jnp.int32, (1, tq), 1)

    tk = sa_ref.shape[0]
    wlen = sw_ref.shape[0]
    wstart = pl.multiple_of(jnp.maximum(t0 + tq - wlen, 0), tq)

    def scores(kt):
        return _dot_nt(ks_ref[pl.ds(pl.multiple_of(kt * tk, tk), tk), :], qs)

    s_cmp = _dot_nt(kc_ref[...], qs)
    sw_ref[...] = _dot_nt(kw_ref[pl.ds(wstart, wlen), :], qs)
    sa_ref[...] = scores(0)

    n_pad = kc_ref.shape[0]
    crow = lax.broadcasted_iota(jnp.int32, (n_pad, tq), 0)
    cmp_ok = (crow * CMP_STRIDE + (CMP_LEN - 1) <= tq_pos) & (crow < n_cmp)
    s = _mask_block_rows(s_cmp, cmp_ok, tq)
    e = jnp.exp2(s - jnp.max(s, axis=0, keepdims=True))
    has_block = jnp.concatenate([tq_pos >= CMP_LEN - 1] * H, axis=1)
    p = e * jnp.where(has_block, 1.0 / jnp.sum(e, axis=0, keepdims=True), 0.0)
    o_cmp = _dot(vct_ref[...], p.astype(BF16))
    psum = p[:, 0:tq]
    for h in range(1, H):
        psum = psum + p[:, h * tq:(h + 1) * tq]

    p_hi, p_lo = _split2(psum)
    ovt = ovt_ref[...]
    imp = _dot(ovt, p_hi) + _dot(ovt, p_lo)
    blk = lax.broadcasted_iota(jnp.int32, (n_blk, tq), 0)
    tlane = t0 + lax.broadcasted_iota(jnp.int32, (n_blk, tq), 1)
    dist = _head_of(tlane) - blk
    forced = (blk == 0) | ((dist >= 0) & (dist < N_LOCAL_SEL))
    causal_blk = blk * SEL_LEN <= tlane
    imp = jnp.where(causal_blk, imp + jnp.where(forced, FORCE_BONUS, 0.0), MASK_VALUE)
    sub = 8
    row_in_group = lax.broadcasted_iota(jnp.int32, (sub, tq), 0)
    groups = [imp[v * sub:(v + 1) * sub, :] for v in range(n_blk // sub)]
    rank = [jnp.zeros((sub, tq), F32) for _ in groups]
    for j in range(n_blk):
        rj = imp[j:j + 1, :]
        for v, blk_imp in enumerate(groups):
            if v * sub > j:
                ahead = jnp.where(rj >= blk_imp, 1.0, 0.0)
            elif (v + 1) * sub <= j:
                ahead = jnp.where(rj > blk_imp, 1.0, 0.0)
            else:
                ahead = jnp.where(row_in_group > j - v * sub,
                                  jnp.where(rj >= blk_imp, 1.0, 0.0), jnp.where(rj > blk_imp, 1.0, 0.0))
            rank[v] = rank[v] + ahead
    rank = jnp.concatenate(rank, axis=0)
    thr_ref[...] = jnp.where(rank < sel_k, (tlane - blk * SEL_LEN).astype(F32), -1.0)

    blocks_per_tile = tk // SEL_LEN
    n_kt = (t0 + tq + tk - 1) // tk
    last = n_kt - 1
    r_in_blk = lax.broadcasted_iota(jnp.int32, (SEL_LEN, tq), 0).astype(F32)

    def weighted_values(kt, e_buf):
        return _dot(vst_ref[:, pl.ds(pl.multiple_of(kt * tk, tk), tk)], e_buf[...])

    def softmax_step(kt, valid, s_buf, e_buf, m):
        rows = []
        for j in range(blocks_per_tile):
            limit = thr_ref[pl.ds(kt * blocks_per_tile + j, 1), :]
            if valid is not None:
                limit = jnp.where(valid, limit, -1.0)
            rows.append(_mask_block_rows(s_buf[j * SEL_LEN:(j + 1) * SEL_LEN, :], r_in_blk <= limit, tq))
        s = jnp.concatenate(rows, axis=0)
        m_new = jnp.maximum(m, jnp.max(s, axis=0, keepdims=True))
        e_buf[...] = jnp.exp2((s - m_new).astype(BF16))
        return m_new, jnp.exp2(m - m_new)

    eb_ref[...] = jnp.zeros_like(eb_ref)

    def body(i, carry):
        m, acc, alpha_b = carry
        kt_a = 2 * i
        kt_b = kt_a + 1
        pv_b = weighted_values(jnp.maximum(kt_a - 1, 0), eb_ref)
        sb_ref[...] = scores(jnp.minimum(kt_b, last))
        s_a_next = scores(jnp.minimum(kt_a + 2, last))
        m, alpha_a = softmax_step(kt_a, None, sa_ref, ea_ref, m)
        sa_ref[...] = s_a_next
        acc = alpha_b * acc + pv_b
        pv_a = weighted_values(kt_a, ea_ref)
        m, alpha_b = softmax_step(jnp.minimum(kt_b, last), kt_b <= last, sb_ref, eb_ref, m)
        acc = alpha_a * acc + pv_a
        return m, acc, alpha_b

    n_pairs = (n_kt + 1) // 2
    init = (jnp.full((1, nl), MASK_VALUE, F32), jnp.zeros((D + V_AUG_ROWS, nl), F32), jnp.zeros((1, nl), F32))
    _, acc, alpha_b = lax.fori_loop(0, n_pairs, body, init)
    acc = alpha_b * acc + weighted_values(jnp.minimum(2 * n_pairs - 1, last), eb_ref)
    o_slc = acc[0:D] * (1.0 / acc[D:D + 1])

    rel = tq_pos - wstart
    wrow = lax.broadcasted_iota(jnp.int32, (wlen, tq), 0)
    head_rows = (wrow[0:tq] <= rel) & (wrow[0:tq] > rel - WINDOW)
    s = jnp.concatenate([_mask_block_rows(sw_ref[0:tq, :], head_rows, tq),
                         _mask_block_rows(sw_ref[tq:wlen, :], wrow[tq:wlen] <= rel, tq)], axis=0)
    e = jnp.exp2((s - jnp.max(s, axis=0, keepdims=True)).astype(BF16))
    acc = _dot(vwt_ref[:, pl.ds(wstart, wlen)], e)
    o_win = acc[0:D] * (1.0 / acc[D:D + 1])

    gt_ref[...] = jnp.transpose(_sigmoid(gate_ref[...]))

    def gate_row(branch):
        base = (GATE_OFF - LANES) + branch * NSA_HEADS + grp * H
        return jnp.concatenate([gt_ref[pl.ds(base + h, 1), :] for h in range(H)], axis=1)

    mix = gate_row(0) * o_cmp + gate_row(1) * o_slc + gate_row(2) * o_win
    nat = jnp.transpose(jnp.concatenate([mix[:, h * tq:(h + 1) * tq] for h in range(H)], axis=0))
    zb = zb_ref[...]
    o_ref[...] = (nat * (zb * _sigmoid(zb))).astype(BF16)


def _nsa_attn(q_heads, kc, vct, kv, proj, batch, seq, ovt, n_cmp, sel_k, tq=Q_TILE):
    t = batch * seq
    n_q = seq // tq
    assert tq == LANES, "the gate block is transposed as one 128 x 128 tile"
    tk = KEY_TILE if seq >= 2 * KEY_TILE else seq // 2
    wlen = WINDOW + tq if seq >= WINDOW + tq else seq
    ks, vst, kw, vwt = kv
    g, h, d = NSA_GROUPS, HEADS_PER_GROUP, HEAD_DIM
    row = lambda b, gi, i: b * n_q + i
    whole = lambda a: pl.BlockSpec((None, None) + a.shape[2:], lambda b, gi, i: (b, gi, 0, 0))
    kc = kc.reshape((batch, g) + kc.shape[1:])
    vct = vct.reshape((batch, g) + vct.shape[1:])
    return pl.pallas_call(
        functools.partial(_nsa_attn_kernel, n_cmp, sel_k),
        grid=(batch, g, n_q),
        in_specs=[
            pl.BlockSpec((None, None, h, tq, d), lambda b, gi, i: (b, gi, 0, i, 0)),
            whole(kc), whole(vct), whole(ks), whole(vst), whole(kw), whole(vwt),
            pl.BlockSpec((tq, LANES), lambda b, gi, i: (row(b, gi, i), (COL_MISC + LANES) // LANES)),
            pl.BlockSpec((tq, MXU_DIM), lambda b, gi, i: (row(b, gi, i), COL_ZB // MXU_DIM + gi)),
            pl.BlockSpec(ovt.shape, lambda b, gi, i: (0, 0)),
        ],
        out_specs=pl.BlockSpec((tq, MXU_DIM), lambda b, gi, i: (row(b, gi, i), gi)),
        out_shape=jax.ShapeDtypeStruct((t, NSA_WIDTH), BF16),
        scratch_shapes=[pltpu.VMEM((ovt.shape[0], tq), F32), pltpu.VMEM((LANES, tq), F32),
                        pltpu.VMEM((tk, h * tq), F32), pltpu.VMEM((tk, h * tq), F32),
                        pltpu.VMEM((tk, h * tq), BF16), pltpu.VMEM((tk, h * tq), BF16),
                        pltpu.VMEM((wlen, h * tq), F32)],
        compiler_params=_params(("parallel", "parallel", "arbitrary")),
    )(q_heads, kc, vct, ks, vst, kw, vwt, proj, proj, ovt)


def _merge_kernel(ya_ref, yb_ref, wa_ref, wb_ref, ga_ref, gb_ref, o_ref):
    pa = _dot(ya_ref[...], wa_ref[...])
    pb = _dot(yb_ref[...], wb_ref[...])
    o_ref[...] = (_sigmoid(ga_ref[...]) * pa + _sigmoid(gb_ref[...]) * pb).astype(BF16)


def _merge(y_a, y_b, w_a, w_b, proj, layer, tm=512):
    t, c = y_a.shape
    yspec = pl.BlockSpec((tm, c), lambda i: (i, 0))
    wspec = pl.BlockSpec((None, c, D_MODEL), lambda i: (layer, 0, 0))
    return pl.pallas_call(
        _merge_kernel,
        grid=(t // tm,),
        in_specs=[yspec, yspec, wspec, wspec,
                  pl.BlockSpec((tm, D_MODEL), lambda i: (i, COL_MERGE // D_MODEL)),
                  pl.BlockSpec((tm, D_MODEL), lambda i: (i, COL_MERGE // D_MODEL + 1))],
        out_specs=pl.BlockSpec((tm, D_MODEL), lambda i: (i, 0)),
        out_shape=jax.ShapeDtypeStruct((t, D_MODEL), BF16),
        compiler_params=_params(("parallel",)),
    )(y_a, y_b, w_a, w_b, proj, proj)


def _outproj_kernel(m_ref, w_ref, g_ref, x_ref, o_ref):
    out = _dot(m_ref[...], w_ref[...])
    ms = jnp.mean(out * out, axis=-1, keepdims=True)
    o_ref[...] = x_ref[...] + out * lax.rsqrt(ms + NORM_EPS) * g_ref[...]


def _outproj(merged, w_out, gain, x, layer, tm=256):
    t = x.shape[0]
    rows = pl.BlockSpec((tm, D_MODEL), lambda i: (i, 0))
    return pl.pallas_call(
        _outproj_kernel,
        grid=(t // tm,),
        in_specs=[rows, pl.BlockSpec((None, D_MODEL, D_MODEL), lambda i: (layer, 0, 0)),
                  pl.BlockSpec((1, D_MODEL), lambda i: (0, 0)), rows],
        out_specs=rows,
        out_shape=jax.ShapeDtypeStruct((t, D_MODEL), F32),
        compiler_params=_params(("parallel",)),
    )(merged, w_out, gain, x)


def _reorder_w_in(w):
    u_end = SHIFT_WIDTH
    za = u_end
    q = za + RWKV_WIDTH
    kv = q + NSA_WIDTH
    gate = kv + 6 * NSA_KV_WIDTH
    zb = gate + 3 * NSA_HEADS
    merge = zb + NSA_WIDTH
    end = merge + 2 * D_MODEL
    lead = w.shape[:-1]
    pieces = [w[..., merge:end], w[..., 0:3 * RWKV_WIDTH], w[..., za:q], w[..., q:kv], w[..., kv:gate],
              w[..., zb:merge], w[..., 3 * RWKV_WIDTH:u_end], w[..., gate:zb],
              jnp.zeros(lead + (MISC_WIDTH - GATE_OFF - 3 * NSA_HEADS,), w.dtype),
              jnp.zeros(lead + (PROJ_WIDTH - COL_MISC - MISC_WIDTH,), w.dtype)]
    return jnp.concatenate([p.astype(BF16) for p in pieces], axis=-1)


def _rope_tables(pos):
    half = ROT_DIM // 2
    inv_freq = ROPE_THETA ** (-jnp.arange(half, dtype=F32) * (2.0 / ROT_DIM))
    ang = pos.astype(F32)[..., None] * inv_freq
    cos, sin = jnp.cos(ang), jnp.sin(ang)
    d = np.arange(LANES) % HEAD_DIM
    idx = d % half
    cos_l, sin_l = cos[..., idx], sin[..., idx]
    cos_t = jnp.where(d < ROT_DIM, cos_l, 1.0)
    sin_up = jnp.where((d >= half) & (d < ROT_DIM), sin_l, 0.0)
    sin_dn = jnp.where(d < half, -sin_l, 0.0)
    return cos_t, sin_up, sin_dn


def _constants(seq):
    n_cmp = (seq - CMP_LEN) // CMP_STRIDE + 1
    n_sel = seq // SEL_LEN
    n_pad = seq // CMP_STRIDE
    ones_bd = (np.arange(MXU_DIM)[:, None] // HEAD_DIM == np.arange(MXU_DIM)[None, :] // HEAD_DIM)
    sel = np.stack([np.arange(MXU_DIM)[:, None] == i * HEAD_DIM + np.arange(HEAD_DIM)[None, :]
                    for i in range(NSA_GROUPS)])
    ci = np.arange(n_pad)[:, None] * CMP_STRIDE
    sj = np.arange(n_sel)[None, :] * SEL_LEN
    overlap = (ci < sj + SEL_LEN) & (ci + CMP_LEN > sj) & (np.arange(n_pad)[:, None] < n_cmp)
    to_bf = lambda a: jnp.asarray(a.astype(np.float32), BF16)
    return dict(ones_bd=to_bf(ones_bd), sel=to_bf(sel), sel_t=to_bf(sel.transpose(0, 2, 1)),
                overlap_t=to_bf(overlap.T)), n_cmp, n_sel


def _pad_rows(w, rows, at):
    out = jnp.zeros((rows, w.shape[1]), w.dtype)
    return out.at[at:at + w.shape[0]].set(w)


def kernel(x, positions, norm_pre, norm_post, w_in, rwkv_mu, rwkv_w0, rwkv_w_up, rwkv_a0, rwkv_a_up,
           rwkv_k_k, rwkv_k_a, rwkv_r_k, rwkv_gn_w, rwkv_gn_b, rwkv_v0, rwkv_v_down, rwkv_v_up,
           nsa_pe_k, nsa_pe_v, nsa_ck_w1, nsa_ck_w2, nsa_cv_w1, nsa_cv_w2, w_proj_a, w_proj_b, w_out):
    batch, seq, d_model = x.shape
    depth = w_in.shape[0]
    t = batch * seq
    c = RWKV_WIDTH
    consts, n_cmp, n_sel = _constants(seq)
    sel_k = min(SEL_TOPK, n_sel)
    n_pad = seq // CMP_STRIDE

    tables = _rope_tables(positions.reshape(t))
    cmp_pos = positions[:, CMP_LEN - 1::CMP_STRIDE]
    cmp_pos = jnp.concatenate([cmp_pos, cmp_pos[:, -1:]], axis=1)
    cmp_tables = _rope_tables(cmp_pos)

    row = lambda a: a.reshape(1, -1).astype(F32)
    xf = x.reshape(t, d_model)
    v_first = None
    w_in_b = _reorder_w_in(w_in)
    w_a_b, w_b_b, w_out_b = w_proj_a.astype(BF16), w_proj_b.astype(BF16), w_out.astype(BF16)
    for l in range(depth):
        proj = _inproj(xf, row(norm_pre[l]), w_in_b, l)

        mu = rwkv_mu[l]
        rp = dict(
            mu_r=row(mu[0:c]), mu_k=row(mu[c:2 * c]), mu_v=row(mu[2 * c:3 * c]),
            mu_m=row(jnp.concatenate([mu[3 * c:], jnp.zeros((MISC_WIDTH - GATE_OFF,), F32)])),
            w0=row(rwkv_w0[l]), a0=row(rwkv_a0[l]), k_k=row(rwkv_k_k[l]), k_a=row(rwkv_k_a[l]),
            r_k=row(rwkv_r_k[l]),
            w_up=_pad_rows(rwkv_w_up[l], MISC_WIDTH, 0).astype(BF16),
            a_up=_pad_rows(rwkv_a_up[l], MISC_WIDTH, DECAY_LORA).astype(BF16),
            ones_bd=consts['ones_bd'], gn_w=row(rwkv_gn_w[l]), gn_b=row(rwkv_gn_b[l]))
        vres = None
        if l > 0:
            vres = dict(v0=row(rwkv_v0[l - 1]),
                        v_dn=jnp.pad(rwkv_v_down[l - 1], ((0, 0), (0, LANES - VRES_LORA))).astype(BF16),
                        v_up=_pad_rows(rwkv_v_up[l - 1], LANES, 0).astype(BF16))
        y_a, vf = _wkv(proj, batch, seq, rp, vres, v_first)
        if l == 0:
            v_first = vf

        q_heads, ks, vst, kw, vwt, kcc, vcc = _nsa_prep(proj, batch, seq, tables, consts['sel'],
                                                        consts['sel_t'])
        cmp_rows = lambda a: a.reshape(batch * NSA_GROUPS, n_pad, CMP_STRIDE * HEAD_DIM)
        cp = dict(pe_k=nsa_pe_k[l].reshape(1, -1), pe_v=nsa_pe_v[l].reshape(1, -1),
                  ck_w1=nsa_ck_w1[l].astype(BF16),
                  ck_w2=jnp.tile(nsa_ck_w2[l], (1, LANES // HEAD_DIM)).astype(BF16),
                  cv_w1=nsa_cv_w1[l].astype(BF16), cv_w2t=nsa_cv_w2[l].T.astype(BF16))
        kc, vct = _compress(cmp_rows(kcc), cmp_rows(vcc), cp, cmp_tables, n_cmp)
        y_b = _nsa_attn(q_heads, kc, vct, (ks, vst, kw, vwt), proj, batch, seq, consts['overlap_t'],
                        n_cmp, sel_k)

        merged = _merge(y_a, y_b, w_a_b, w_b_b, proj, l)
        xf = _outproj(merged, w_out_b, row(norm_post[l]), xf, l)
    return xf.reshape(batch, seq, d_model)
```

```python
import functools
import math

import numpy as np
import jax
import jax.numpy as jnp
from jax import lax
from jax.experimental import pallas as pl
from jax.experimental.pallas import tpu as pltpu

F32 = jnp.float32
BF16 = jnp.bfloat16

D_MODEL = 2048
RWKV_HEADS = 16
HEAD_DIM = 64
RWKV_WIDTH = RWKV_HEADS * HEAD_DIM
DECAY_LORA = 96
ICLR_LORA = 96
VRES_LORA = 64
GN_EPS = 64e-5
L2_EPS = 1e-12
SHIFT_WIDTH = 3 * RWKV_WIDTH + DECAY_LORA + ICLR_LORA

NSA_HEADS = 16
NSA_GROUPS = 4
HEADS_PER_GROUP = NSA_HEADS // NSA_GROUPS
NSA_WIDTH = NSA_HEADS * HEAD_DIM
NSA_KV_WIDTH = NSA_GROUPS * HEAD_DIM
CMP_LEN = 32
CMP_STRIDE = 16
CMP_HIDDEN = 256
SEL_LEN = 64
SEL_TOPK = 16
N_LOCAL_SEL = 2
WINDOW = 512
ROPE_THETA = 500000.0
ROT_DIM = HEAD_DIM // 4
FORCE_BONUS = 1e3
MASK_VALUE = -1e30
NORM_EPS = 1e-6

LANES = 128
MXU_DIM = 256
VMEM_LIMIT = 56 * 1024 * 1024

COL_MERGE = 0
COL_R = 4096
COL_K = 5120
COL_V = 6144
COL_ZA = 7168
COL_Q = 8192
COL_KV = 9216
COL_ZB = 10752
COL_MISC = 11776
MISC_WIDTH = 256
GATE_OFF = DECAY_LORA + ICLR_LORA
PROJ_WIDTH = 12288

WKV_CHUNK = 64
WKV_BATCH = 2
HEAD_GROUP_LANES = 256
Q_TILE = 128
ATTN_GROUPS = 2
KEY_TILE = 256
V_AUG_ROWS = 16
LOG2E = 1.4426950408889634


def _dot(a, b):
    return jnp.dot(a, b, preferred_element_type=F32)


def _dot_nt(a, b):
    return lax.dot_general(a, b, (((1,), (1,)), ((), ())), preferred_element_type=F32)


def _dot_tn(a, b):
    return lax.dot_general(a, b, (((0,), (0,)), ((), ())), preferred_element_type=F32)


def _split2(x):
    hi = x.astype(BF16)
    lo = (x - hi.astype(F32)).astype(BF16)
    return hi, lo


def _dot2(x, w):
    hi, lo = _split2(x)
    return _dot(hi, w) + _dot(lo, w)


def _head_sum(x, ones_bd):
    width = ones_bd.shape[0]
    parts = []
    for c in range(x.shape[1] // width):
        parts.append(_dot2(x[:, c * width:(c + 1) * width], ones_bd))
    return jnp.concatenate(parts, axis=1)


def _sigmoid(x):
    return 1.0 / (1.0 + jnp.exp(-x))


def _head_of(idx):
    return jnp.right_shift(idx, int(math.log2(HEAD_DIM)))


def _params(sem):
    return pltpu.CompilerParams(dimension_semantics=sem, vmem_limit_bytes=VMEM_LIMIT)


def _inproj_kernel(x_ref, g_ref, w_ref, o_ref, h_ref):
    @pl.when(pl.program_id(1) == 0)
    def _():
        x = x_ref[...]
        ms = jnp.mean(x * x, axis=-1, keepdims=True)
        h_ref[...] = (x * lax.rsqrt(ms + NORM_EPS) * g_ref[...]).astype(BF16)

    o_ref[...] = _dot(h_ref[...], w_ref[...])


def _inproj(x, gain, w, layer, tm=1024, tn=1024):
    t = x.shape[0]
    return pl.pallas_call(
        _inproj_kernel,
        grid=(t // tm, PROJ_WIDTH // tn),
        in_specs=[
            pl.BlockSpec((tm, D_MODEL), lambda i, j: (i, 0)),
            pl.BlockSpec((1, D_MODEL), lambda i, j: (0, 0)),
            pl.BlockSpec((None, D_MODEL, tn), lambda i, j: (layer, 0, j)),
        ],
        out_specs=pl.BlockSpec((tm, tn), lambda i, j: (i, j)),
        out_shape=jax.ShapeDtypeStruct((t, PROJ_WIDTH), F32),
        scratch_shapes=[pltpu.VMEM((tm, D_MODEL), BF16)],
        compiler_params=_params(("parallel", "arbitrary")),
    )(x, gain, w)


def _wkv_kernel(has_vres, *refs):
    (r_ref, k_ref, v_ref, m_ref, za_ref, mu_r, mu_k, mu_v, mu_m, w0_ref, a0_ref, kk_w, ka_w, rk_w,
     wup_ref, aup_ref, gnw_ref, gnb_ref, ones_ref) = refs[:19]
    pos = 19
    if has_vres:
        v0_ref, vdn_ref, vup_ref, vf_ref = refs[pos:pos + 4]
        pos += 4
        o_ref = refs[pos]
        pos += 1
    else:
        o_ref, vf_out = refs[pos:pos + 2]
        pos += 2
    st_ref, lr_ref, lk_ref, lv_ref, lm_ref = refs[pos:pos + 5]

    nb, L, width = r_ref.shape
    W = HEAD_GROUP_LANES
    n_groups = width // W
    first_chunk = pl.program_id(1) == 0

    @pl.when(first_chunk)
    def _():
        st_ref[...] = jnp.zeros_like(st_ref)

    rows = nb * L
    ti = lax.broadcasted_iota(jnp.int32, (rows, rows), 0)
    tj = lax.broadcasted_iota(jnp.int32, (rows, rows), 1)
    tri = jnp.where((_head_of(ti) == _head_of(tj)) & (ti >= tj), 1.0, 0.0).astype(BF16)
    ri = lax.broadcasted_iota(jnp.int32, (W, W), 0)
    ci = lax.broadcasted_iota(jnp.int32, (W, W), 1)
    to_bf = lambda idx: idx.astype(F32).astype(BF16)
    same_head = to_bf(_head_of(ri)) == to_bf(_head_of(ci))
    row_id = lax.broadcasted_iota(jnp.int32, (rows, 1), 0)
    ones_bd = ones_ref[...]
    tail = lr_ref.shape[1]

    def stack4(a):
        return jnp.concatenate([a] * 4, axis=0)

    def xform(a):
        return jnp.where(same_head, stack4(a.astype(BF16)), 0.0)

    def shifted(u_ref, last_ref, mu_ref):
        u = u_ref[...].reshape(rows, u_ref.shape[2])
        prev = pltpu.roll(u, 1, axis=0)
        for bi in range(nb):
            last = last_ref[bi, tail - 1:tail, :]
            last = jnp.where(first_chunk, jnp.zeros_like(last), last)
            prev = jnp.where(row_id == bi * L, last, prev)
            last_ref[bi] = u[(bi + 1) * L - tail:(bi + 1) * L, :]
        return u + (prev - u) * mu_ref[...]

    r = shifted(r_ref, lr_ref, mu_r)
    k = shifted(k_ref, lk_ref, mu_k)
    v = shifted(v_ref, lv_ref, mu_v)
    misc = shifted(m_ref, lm_ref, mu_m)
    z = -(w0_ref[...] + _dot(jnp.tanh(misc).astype(BF16), wup_ref[...]))
    softplus = jnp.maximum(z, 0.0) + jnp.log(1.0 + jnp.exp(-jnp.abs(z)))
    lw = -jnp.exp(-softplus - 0.5)
    a = _sigmoid(a0_ref[...] + _dot(misc.astype(BF16), aup_ref[...]))
    if has_vres:
        low = _dot(v.astype(BF16), vdn_ref[...])
        gate = _sigmoid(v0_ref[...] + _dot(low.astype(BF16), vup_ref[...]))
        v = v + (vf_ref[...].reshape(rows, width) - v) * gate
    else:
        vf_out[...] = v.reshape(nb, L, width)
    kk = k * kk_w[...]
    kk = kk / jnp.maximum(jnp.sqrt(_head_sum(kk * kk, ones_bd)), L2_EPS)
    k = k * (1.0 + (a - 1.0) * ka_w[...])
    b = kk * a
    bonus = _head_sum(r * k * rk_w[...], ones_bd) * v

    lw_hi, lw_lo = _split2(lw)
    cum = _dot(tri, lw_hi) + _dot(tri, lw_lo)
    ab_all = -(jnp.exp(cum - lw) * kk)
    rb_all = jnp.exp(cum) * r

    abx, rbc, vx, kpx, bpx, bk, pcol, ar = [], [], [], [], [], [], [], []
    for bi in range(nb):
        rs = slice(bi * L, (bi + 1) * L)
        cum_b = cum[rs]
        pend = cum_b[L - 1:L, :]
        e_n = jnp.exp(-cum_b)
        e_e = jnp.exp(pend - cum_b)
        bb_all = e_n * b[rs]
        kb_all = e_n * k[rs]
        bp_all = e_e * b[rs]
        kp_all = e_e * k[rs]
        p_end = jnp.exp(pend)
        for g in range(n_groups):
            sl = slice(g * W, (g + 1) * W)
            abx.append(xform(ab_all[rs, sl]))
            rbc.append(rb_all[rs, sl].astype(BF16))
            vx.append(xform(v[rs, sl]))
            kpx.append(xform(kp_all[:, sl]))
            bpx.append(xform(bp_all[:, sl]))
            bk.append(jnp.concatenate([xform(bb_all[:, sl]), xform(kb_all[:, sl])], axis=0))
            ar.append(jnp.concatenate([ab_all[rs, sl], rb_all[rs, sl]], axis=0).astype(BF16))
            pcol.append(jnp.transpose(jnp.broadcast_to(p_end[:, sl], (W, W))))
    chains = range(nb * n_groups)

    tc = lax.broadcasted_iota(jnp.int32, (L, W), 0)
    sc = jnp.bitwise_and(lax.broadcasted_iota(jnp.int32, (L, W), 1), HEAD_DIM - 1)
    g = [_dot_nt(ar[c], bk[c]) for c in chains]
    nm = [jnp.where(tc > sc, g[c][0:L, 0:W], 0.0) for c in chains]
    akrk = [jnp.concatenate([jnp.where(tc > sc, g[c][0:L, W:2 * W], 0.0),
                             jnp.where(tc >= sc, g[c][L:2 * L, W:2 * W], 0.0)], axis=0).astype(BF16)
            for c in chains]
    rbm = [jnp.where(tc >= sc, g[c][L:2 * L, 0:W], 0.0).astype(BF16) for c in chains]

    tinv = [jnp.where(tc == sc, 1.0, 0.0) + nm[c] for c in chains]
    pw = [nm[c].astype(BF16) for c in chains]
    for _ in range(5):
        pw = [_dot(pw[c], xform(pw[c])).astype(BF16) for c in chains]
        tinv = [tinv[c] + _dot(pw[c], xform(tinv[c])) for c in chains]
    tb = [tinv[c].astype(BF16) for c in chains]

    akrk_v = [_dot(akrk[c], vx[c]) for c in chains]
    kv = [_dot_tn(kpx[c], vx[c]) for c in chains]
    wu = [_dot(tb[c], jnp.concatenate([abx[c], xform(akrk_v[c][0:L])], axis=1)) for c in chains]

    st = [st_ref[c] for c in chains]
    stb = [st[c].astype(BF16) for c in chains]
    ub = [xform(_dot(wu[c][:, 0:W].astype(BF16), stb[c]) + wu[c][:, W:2 * W]) for c in chains]
    for c in chains:
        st_ref[c] = pcol[c] * st[c] + _dot_tn(bpx[c], ub[c]) + kv[c]
    yc = [_dot(jnp.concatenate([rbc[c], rbm[c]], axis=1), jnp.concatenate([stb[c], ub[c]], axis=0))
          + akrk_v[c][L:2 * L] for c in chains]

    inv_n = 1.0 / HEAD_DIM
    per_elem = []
    for bi in range(nb):
        per_elem.append(jnp.concatenate([yc[bi * n_groups + gi] for gi in range(n_groups)], axis=1))
    y = jnp.concatenate(per_elem, axis=0)
    mean = _head_sum(y, ones_bd) * inv_n
    d = y - mean
    var = _head_sum(d * d, ones_bd) * inv_n
    yn = d * lax.rsqrt(var + GN_EPS) * gnw_ref[...] + gnb_ref[...]
    za = za_ref[...].reshape(rows, width)
    o_ref[...] = ((yn + bonus) * (za * _sigmoid(za))).astype(BF16).reshape(nb, L, width)


def _wkv(proj, batch, seq, p, vres, v_first, nb=WKV_BATCH):
    c = RWKV_WIDTH
    nb = min(nb, batch)
    n_chunks = seq // WKV_CHUNK
    has_vres = vres is not None
    proj3 = proj.reshape(batch, seq, proj.shape[1])

    def cols(width, col):
        return pl.BlockSpec((nb, WKV_CHUNK, width), lambda bi, ci, _c=col // width: (bi, ci, _c))

    def full(shape):
        return pl.BlockSpec(shape, lambda bi, ci: (0,) * len(shape))

    vec = full((1, c))
    in_specs = [cols(c, COL_R), cols(c, COL_K), cols(c, COL_V), cols(MISC_WIDTH, COL_MISC), cols(c, COL_ZA),
                vec, vec, vec, full((1, MISC_WIDTH)), vec, vec, vec, vec, vec,
                full((MISC_WIDTH, c)), full((MISC_WIDTH, c)), vec, vec,
                full((HEAD_GROUP_LANES, HEAD_GROUP_LANES))]
    args = [proj3, proj3, proj3, proj3, proj3, p['mu_r'], p['mu_k'], p['mu_v'], p['mu_m'], p['w0'], p['a0'],
            p['k_k'], p['k_a'], p['r_k'], p['w_up'], p['a_up'], p['gn_w'], p['gn_b'], p['ones_bd']]
    rows = pl.BlockSpec((nb, WKV_CHUNK, c), lambda bi, ci: (bi, ci, 0))
    y_sds = jax.ShapeDtypeStruct((batch, seq, c), BF16)
    if has_vres:
        in_specs += [vec, full((c, LANES)), full((LANES, c)), rows]
        args += [vres['v0'], vres['v_dn'], vres['v_up'], v_first]
        out_specs, out_shape = rows, y_sds
    else:
        out_specs, out_shape = [rows, rows], [y_sds, jax.ShapeDtypeStruct((batch, seq, c), F32)]
    tail = 8
    out = pl.pallas_call(
        functools.partial(_wkv_kernel, has_vres),
        grid=(batch // nb, n_chunks),
        in_specs=in_specs,
        out_specs=out_specs,
        out_shape=out_shape,
        scratch_shapes=[pltpu.VMEM((nb * c // HEAD_GROUP_LANES, HEAD_GROUP_LANES, HEAD_GROUP_LANES), F32),
                        pltpu.VMEM((nb, tail, c), F32), pltpu.VMEM((nb, tail, c), F32),
                        pltpu.VMEM((nb, tail, c), F32), pltpu.VMEM((nb, tail, MISC_WIDTH), F32)],
        compiler_params=_params(("parallel", "arbitrary")),
    )(*args)
    if has_vres:
        return out.reshape(batch * seq, c), None
    return out[0].reshape(batch * seq, c), out[1]


def _rotary(x, cos_t, sin_up, sin_dn):
    reps = x.shape[1] // LANES
    half = ROT_DIM // 2
    tile = lambda tbl: jnp.concatenate([tbl] * reps, axis=1) if reps > 1 else tbl
    return (x * tile(cos_t) + pltpu.roll(x, half, axis=1) * tile(sin_up)
            + pltpu.roll(x, x.shape[1] - half, axis=1) * tile(sin_dn))


def _nsa_prep_kernel(q_ref, kv_ref, cos_ref, sup_ref, sdn_ref, sel_ref, selt_ref,
                     q_o, ks_o, vst_o, kw_o, vwt_o, kcc_o, vcc_o, stage_ref):
    cos_t, sin_up, sin_dn = cos_ref[...], sup_ref[...], sdn_ref[...]
    q = (_rotary(q_ref[...], cos_t, sin_up, sin_dn) * (HEAD_DIM ** -0.5 * LOG2E)).astype(BF16)
    w = NSA_KV_WIDTH
    kv = kv_ref[...]
    ks = _rotary(kv[:, 2 * w:3 * w], cos_t, sin_up, sin_dn).astype(BF16)
    vs = kv[:, 3 * w:4 * w].astype(BF16)
    kw = _rotary(kv[:, 4 * w:5 * w], cos_t, sin_up, sin_dn).astype(BF16)
    vw = kv[:, 5 * w:6 * w].astype(BF16)
    ones = jnp.ones((V_AUG_ROWS, q.shape[0]), BF16)
    for g in range(NSA_GROUPS):
        qg = q[:, g * MXU_DIM:(g + 1) * MXU_DIM]
        for h in range(HEADS_PER_GROUP):
            q_o[g, h] = _dot(qg, sel_ref[h]).astype(BF16)
        ks_o[g] = _dot(ks, sel_ref[g]).astype(BF16)
        kw_o[g] = _dot(kw, sel_ref[g]).astype(BF16)
        vst_o[g] = jnp.concatenate([_dot_nt(selt_ref[g], vs).astype(BF16), ones], axis=0)
        vwt_o[g] = jnp.concatenate([_dot_nt(selt_ref[g], vw).astype(BF16), ones], axis=0)
    n_stage = stage_ref.shape[0]
    for c in range(n_stage):
        stage_ref[c] = kv[:, c * LANES:(c + 1) * LANES]
    n_rows = kv.shape[0] // CMP_STRIDE
    taps = [[stage_ref[c, pl.ds(m, n_rows, stride=CMP_STRIDE), :] for m in range(CMP_STRIDE)]
            for c in range(n_stage)]
    per_buf = LANES // HEAD_DIM
    for g in range(NSA_GROUPS):
        for out, first in ((kcc_o, 0), (vcc_o, n_stage // 2)):
            c, half = first + g // per_buf, g % per_buf
            out[g] = jnp.concatenate(
                [tap[:, half * HEAD_DIM:(half + 1) * HEAD_DIM] for tap in taps[c]], axis=1).astype(BF16)


def _nsa_prep(proj, batch, seq, tables, sel, sel_t, tr=256):
    n_t = seq // tr
    row = lambda bi, i: bi * n_t + i
    tbl = pl.BlockSpec((tr, LANES), lambda bi, i: (row(bi, i), 0))
    g, h, d = NSA_GROUPS, HEADS_PER_GROUP, HEAD_DIM
    k_o = pl.BlockSpec((None, g, tr, d), lambda bi, i: (bi, 0, i, 0))
    vt_o = pl.BlockSpec((None, g, d + V_AUG_ROWS, tr), lambda bi, i: (bi, 0, 0, i))
    k_sds = jax.ShapeDtypeStruct((batch, g, seq, d), BF16)
    vt_sds = jax.ShapeDtypeStruct((batch, g, d + V_AUG_ROWS, seq), BF16)
    cc_o = pl.BlockSpec((None, g, tr // CMP_STRIDE, CMP_STRIDE * d), lambda bi, i: (bi, 0, i, 0))
    cc_sds = jax.ShapeDtypeStruct((batch, g, seq // CMP_STRIDE, CMP_STRIDE * d), BF16)
    return pl.pallas_call(
        _nsa_prep_kernel,
        grid=(batch, n_t),
        in_specs=[
            pl.BlockSpec((tr, NSA_WIDTH), lambda bi, i: (row(bi, i), COL_Q // NSA_WIDTH)),
            pl.BlockSpec((tr, 6 * NSA_KV_WIDTH), lambda bi, i: (row(bi, i), COL_KV // (6 * NSA_KV_WIDTH))),
            tbl, tbl, tbl,
            pl.BlockSpec(sel.shape, lambda bi, i: (0, 0, 0)),
            pl.BlockSpec(sel_t.shape, lambda bi, i: (0, 0, 0)),
        ],
        out_specs=[pl.BlockSpec((None, g, h, tr, d), lambda bi, i: (bi, 0, 0, i, 0)), k_o, vt_o, k_o, vt_o,
                   cc_o, cc_o],
        out_shape=[jax.ShapeDtypeStruct((batch, g, h, seq, d), BF16), k_sds, vt_sds, k_sds, vt_sds,
                   cc_sds, cc_sds],
        scratch_shapes=[pltpu.VMEM((2 * NSA_KV_WIDTH // LANES, tr, LANES), F32)],
        compiler_params=_params(("parallel", "parallel")),
    )(proj, proj, tables[0], tables[1], tables[2], sel, sel_t)


def _compress_kernel(n_cmp, kc_ref, vc_ref, pek_ref, pev_ref, kw1_ref, kw2_ref, vw1_ref, vw2t_ref,
                     cos_ref, sup_ref, sdn_ref, ko_ref, vto_ref):
    half = kw1_ref.shape[0] // 2
    rows = kc_ref.shape[0]

    def hidden(c_ref, pe_ref, w1_ref):
        c = c_ref[...].astype(BF16)
        pe = jnp.broadcast_to(pe_ref[...], (8, 2 * half)).astype(BF16)
        h = (_dot(c, w1_ref[0:half, :])
             + pltpu.roll(_dot(c, w1_ref[half:2 * half, :]), rows - 1, axis=0)
             + _dot(pe, w1_ref[...])[0:1, :])
        inner = math.sqrt(2.0 / math.pi) * (h + 0.044715 * (h * h * h))
        return (0.5 * h * (1.0 + jnp.tanh(inner))).astype(BF16)

    kc = _rotary(_dot(hidden(kc_ref, pek_ref, kw1_ref), kw2_ref[...]),
                 cos_ref[...], sup_ref[...], sdn_ref[...])
    valid_row = lax.broadcasted_iota(jnp.int32, (rows, 1), 0) < n_cmp
    ko_ref[...] = jnp.where(valid_row, kc[:, 0:HEAD_DIM], 0.0).astype(BF16)
    vct = _dot_nt(vw2t_ref[...], hidden(vc_ref, pev_ref, vw1_ref))
    valid_col = lax.broadcasted_iota(jnp.int32, (1, rows), 1) < n_cmp
    vto_ref[...] = jnp.where(valid_col, vct, 0.0).astype(BF16)


def _compress(kc_in, vc_in, p, cmp_tables, n_cmp):
    bg, rows, width = kc_in.shape
    groups = NSA_GROUPS
    cin = pl.BlockSpec((None, rows, width), lambda i: (i, 0, 0))
    full = lambda shape: pl.BlockSpec(shape, lambda i: (0,) * len(shape))
    tbl = pl.BlockSpec((None, rows, LANES), lambda i: (i // groups, 0, 0))
    return pl.pallas_call(
        functools.partial(_compress_kernel, n_cmp),
        grid=(bg,),
        in_specs=[cin, cin, full((1, 2 * width)), full((1, 2 * width)),
                  full((2 * width, CMP_HIDDEN)), full((CMP_HIDDEN, LANES)),
                  full((2 * width, CMP_HIDDEN)), full((HEAD_DIM, CMP_HIDDEN)),
                  tbl, tbl, tbl],
        out_specs=[pl.BlockSpec((None, rows, HEAD_DIM), lambda i: (i, 0, 0)),
                   pl.BlockSpec((None, HEAD_DIM, rows), lambda i: (i, 0, 0))],
        out_shape=[jax.ShapeDtypeStruct((bg, rows, HEAD_DIM), BF16),
                   jax.ShapeDtypeStruct((bg, HEAD_DIM, rows), BF16)],
        compiler_params=_params(("parallel",)),
    )(kc_in, vc_in, p['pe_k'], p['pe_v'], p['ck_w1'], p['ck_w2'], p['cv_w1'], p['cv_w2t'], *cmp_tables)


def _mask_block_rows(s, allowed, tq):
    heads = s.shape[1] // tq
    return jnp.concatenate(
        [jnp.where(allowed, s[:, h * tq:(h + 1) * tq], MASK_VALUE) for h in range(heads)], axis=1)


def _nsa_attn_kernel(n_cmp, sel_k, q_ref, kc_ref, vct_ref, ks_ref, vst_ref, kw_ref, vwt_ref,
                     gate_ref, zb_ref, ovt_ref, o_ref, thr_ref, gt_ref, sa_ref, sb_ref, ea_ref, eb_ref,
                     sw_ref):
    H = HEADS_PER_GROUP
    D = HEAD_DIM
    tq = q_ref.shape[1]
    nl = H * tq
    n_blk = ovt_ref.shape[0]
    grp = pl.program_id(1)
    t0 = pl.program_id(2) * tq

    qs = q_ref[...].reshape(nl, D)
    tq_pos = t0 + lax.broadcasted_iota(jnp.int32, (1, tq), 1)

    tk = sa_ref.shape[0]
    wlen = sw_ref.shape[0]
    wstart = pl.multiple_of(jnp.maximum(t0 + tq - wlen, 0), tq)

    def scores(kt):
        return _dot_nt(ks_ref[pl.ds(pl.multiple_of(kt * tk, tk), tk), :], qs)

    s_cmp = _dot_nt(kc_ref[...], qs)
    sw_ref[...] = _dot_nt(kw_ref[pl.ds(wstart, wlen), :], qs)
    sa_ref[...] = scores(0)

    n_pad = kc_ref.shape[0]
    crow = lax.broadcasted_iota(jnp.int32, (n_pad, tq), 0)
    cmp_ok = (crow * CMP_STRIDE + (CMP_LEN - 1) <= tq_pos) & (crow < n_cmp)
    s = _mask_block_rows(s_cmp, cmp_ok, tq)
    e = jnp.exp2(s - jnp.max(s, axis=0, keepdims=True))
    has_block = jnp.concatenate([tq_pos >= CMP_LEN - 1] * H, axis=1)
    p = e * jnp.where(has_block, 1.0 / jnp.sum(e, axis=0, keepdims=True), 0.0)
    o_cmp = _dot(vct_ref[...], p.astype(BF16))
    psum = p[:, 0:tq]
    for h in range(1, H):
        psum = psum + p[:, h * tq:(h + 1) * tq]

    p_hi, p_lo = _split2(psum)
    ovt = ovt_ref[...]
    imp = _dot(ovt, p_hi) + _dot(ovt, p_lo)
    blk = lax.broadcasted_iota(jnp.int32, (n_blk, tq), 0)
    tlane = t0 + lax.broadcasted_iota(jnp.int32, (n_blk, tq), 1)
    dist = _head_of(tlane) - blk
    forced = (blk == 0) | ((dist >= 0) & (dist < N_LOCAL_SEL))
    causal_blk = blk * SEL_LEN <= tlane
    imp = jnp.where(causal_blk, imp + jnp.where(forced, FORCE_BONUS, 0.0), MASK_VALUE)
    sub = 8
    row_in_group = lax.broadcasted_iota(jnp.int32, (sub, tq), 0)
    groups = [imp[v * sub:(v + 1) * sub, :] for v in range(n_blk // sub)]
    rank = [jnp.zeros((sub, tq), F32) for _ in groups]
    for j in range(n_blk):
        rj = imp[j:j + 1, :]
        for v, blk_imp in enumerate(groups):
            if v * sub > j:
                ahead = jnp.where(rj >= blk_imp, 1.0, 0.0)
            elif (v + 1) * sub <= j:
                ahead = jnp.where(rj > blk_imp, 1.0, 0.0)
            else:
                ahead = jnp.where(row_in_group > j - v * sub,
                                  jnp.where(rj >= blk_imp, 1.0, 0.0), jnp.where(rj > blk_imp, 1.0, 0.0))
            rank[v] = rank[v] + ahead
    rank = jnp.concatenate(rank, axis=0)
    thr_ref[...] = jnp.where(rank < sel_k, (tlane - blk * SEL_LEN).astype(F32), -1.0)

    blocks_per_tile = tk // SEL_LEN
    n_kt = (t0 + tq + tk - 1) // tk
    last = n_kt - 1
    r_in_blk = lax.broadcasted_iota(jnp.int32, (SEL_LEN, tq), 0).astype(F32)

    def weighted_values(kt, e_buf):
        return _dot(vst_ref[:, pl.ds(pl.multiple_of(kt * tk, tk), tk)], e_buf[...])

    def softmax_step(kt, valid, s_buf, e_buf, m):
        rows = []
        for j in range(blocks_per_tile):
            limit = thr_ref[pl.ds(kt * blocks_per_tile + j, 1), :]
            if valid is not None:
                limit = jnp.where(valid, limit, -1.0)
            rows.append(_mask_block_rows(s_buf[j * SEL_LEN:(j + 1) * SEL_LEN, :], r_in_blk <= limit, tq))
        s = jnp.concatenate(rows, axis=0)
        m_new = jnp.maximum(m, jnp.max(s, axis=0, keepdims=True))
        e_buf[...] = jnp.exp2((s - m_new).astype(BF16))
        return m_new, jnp.exp2(m - m_new)

    eb_ref[...] = jnp.zeros_like(eb_ref)

    def body(i, carry):
        m, acc, alpha_b = carry
        kt_a = 2 * i
        kt_b = kt_a + 1
        pv_b = weighted_values(jnp.maximum(kt_a - 1, 0), eb_ref)
        sb_ref[...] = scores(jnp.minimum(kt_b, last))
        s_a_next = scores(jnp.minimum(kt_a + 2, last))
        m, alpha_a = softmax_step(kt_a, None, sa_ref, ea_ref, m)
        sa_ref[...] = s_a_next
        acc = alpha_b * acc + pv_b
        pv_a = weighted_values(kt_a, ea_ref)
        m, alpha_b = softmax_step(jnp.minimum(kt_b, last), kt_b <= last, sb_ref, eb_ref, m)
        acc = alpha_a * acc + pv_a
        return m, acc, alpha_b

    n_pairs = (n_kt + 1) // 2
    init = (jnp.full((1, nl), MASK_VALUE, F32), jnp.zeros((D + V_AUG_ROWS, nl), F32), jnp.zeros((1, nl), F32))
    _, acc, alpha_b = lax.fori_loop(0, n_pairs, body, init)
    acc = alpha_b * acc + weighted_values(jnp.minimum(2 * n_pairs - 1, last), eb_ref)
    o_slc = acc[0:D] * (1.0 / acc[D:D + 1])

    rel = tq_pos - wstart
    wrow = lax.broadcasted_iota(jnp.int32, (wlen, tq), 0)
    head_rows = (wrow[0:tq] <= rel) & (wrow[0:tq] > rel - WINDOW)
    s = jnp.concatenate([_mask_block_rows(sw_ref[0:tq, :], head_rows, tq),
                         _mask_block_rows(sw_ref[tq:wlen, :], wrow[tq:wlen] <= rel, tq)], axis=0)
    e = jnp.exp2((s - jnp.max(s, axis=0, keepdims=True)).astype(BF16))
    acc = _dot(vwt_ref[:, pl.ds(wstart, wlen)], e)
    o_win = acc[0:D] * (1.0 / acc[D:D + 1])

    gt_ref[...] = jnp.transpose(_sigmoid(gate_ref[...]))

    def gate_row(branch):
        base = (GATE_OFF - LANES) + branch * NSA_HEADS + grp * H
        return jnp.concatenate([gt_ref[pl.ds(base + h, 1), :] for h in range(H)], axis=1)

    mix = gate_row(0) * o_cmp + gate_row(1) * o_slc + gate_row(2) * o_win
    nat = jnp.transpose(jnp.concatenate([mix[:, h * tq:(h + 1) * tq] for h in range(H)], axis=0))
    zb = zb_ref[...]
    o_ref[...] = (nat * (zb * _sigmoid(zb))).astype(BF16)


def _nsa_attn(q_heads, kc, vct, kv, proj, batch, seq, ovt, n_cmp, sel_k, tq=Q_TILE):
    t = batch * seq
    n_q = seq // tq
    assert tq == LANES, "the gate block is transposed as one 128 x 128 tile"
    tk = KEY_TILE if seq >= 2 * KEY_TILE else seq // 2
    wlen = WINDOW + tq if seq >= WINDOW + tq else seq
    ks, vst, kw, vwt = kv
    g, h, d = NSA_GROUPS, HEADS_PER_GROUP, HEAD_DIM
    row = lambda b, gi, i: b * n_q + i
    whole = lambda a: pl.BlockSpec((None, None) + a.shape[2:], lambda b, gi, i: (b, gi, 0, 0))
    kc = kc.reshape((batch, g) + kc.shape[1:])
    vct = vct.reshape((batch, g) + vct.shape[1:])
    return pl.pallas_call(
        functools.partial(_nsa_attn_kernel, n_cmp, sel_k),
        grid=(batch, g, n_q),
        in_specs=[
            pl.BlockSpec((None, None, h, tq, d), lambda b, gi, i: (b, gi, 0, i, 0)),
            whole(kc), whole(vct), whole(ks), whole(vst), whole(kw), whole(vwt),
            pl.BlockSpec((tq, LANES), lambda b, gi, i: (row(b, gi, i), (COL_MISC + LANES) // LANES)),
            pl.BlockSpec((tq, MXU_DIM), lambda b, gi, i: (row(b, gi, i), COL_ZB // MXU_DIM + gi)),
            pl.BlockSpec(ovt.shape, lambda b, gi, i: (0, 0)),
        ],
        out_specs=pl.BlockSpec((tq, MXU_DIM), lambda b, gi, i: (row(b, gi, i), gi)),
        out_shape=jax.ShapeDtypeStruct((t, NSA_WIDTH), BF16),
        scratch_shapes=[pltpu.VMEM((ovt.shape[0], tq), F32), pltpu.VMEM((LANES, tq), F32),
                        pltpu.VMEM((tk, h * tq), F32), pltpu.VMEM((tk, h * tq), F32),
                        pltpu.VMEM((tk, h * tq), BF16), pltpu.VMEM((tk, h * tq), BF16),
                        pltpu.VMEM((wlen, h * tq), F32)],
        compiler_params=_params(("parallel", "parallel", "arbitrary")),
    )(q_heads, kc, vct, ks, vst, kw, vwt, proj, proj, ovt)


def _nsa_attn_kernel(n_cmp, sel_k, q_ref, kc_ref, vct_ref, ks_ref, vst_ref, kw_ref, vwt_ref,
                     gate_ref, zb_ref, ovt_ref, o_ref, thr_ref, gt_ref, sa_ref, sb_ref, ea_ref, eb_ref,
                     sw_ref):
    H = HEADS_PER_GROUP
    D = HEAD_DIM
    GP = q_ref.shape[0]
    groups = range(GP)
    tq = q_ref.shape[2]
    nl = H * tq
    n_blk = ovt_ref.shape[0]
    t0 = pl.program_id(2) * tq
    tq_pos = t0 + lax.broadcasted_iota(jnp.int32, (1, tq), 1)
    tk = sa_ref.shape[1]
    wlen = sw_ref.shape[1]
    wstart = pl.multiple_of(jnp.maximum(t0 + tq - wlen, 0), tq)

    qs = [q_ref[gi].reshape(nl, D) for gi in groups]

    def scores(gi, kt):
        return _dot_nt(ks_ref[gi, pl.ds(pl.multiple_of(kt * tk, tk), tk), :], qs[gi])

    s_cmp = [_dot_nt(kc_ref[gi], qs[gi]) for gi in groups]
    for gi in groups:
        sw_ref[gi] = _dot_nt(kw_ref[gi, pl.ds(wstart, wlen), :], qs[gi])
        sa_ref[gi] = scores(gi, 0)

    n_pad = kc_ref.shape[1]
    crow = lax.broadcasted_iota(jnp.int32, (n_pad, tq), 0)
    cmp_ok = (crow * CMP_STRIDE + (CMP_LEN - 1) <= tq_pos) & (crow < n_cmp)
    has_block = jnp.concatenate([tq_pos >= CMP_LEN - 1] * H, axis=1)
    blk = lax.broadcasted_iota(jnp.int32, (n_blk, tq), 0)
    tlane = t0 + lax.broadcasted_iota(jnp.int32, (n_blk, tq), 1)
    dist = _head_of(tlane) - blk
    forced = (blk == 0) | ((dist >= 0) & (dist < N_LOCAL_SEL))
    causal_blk = blk * SEL_LEN <= tlane
    ovt = ovt_ref[...]
    sub = 8
    row_in_group = lax.broadcasted_iota(jnp.int32, (sub, tq), 0)
    o_cmp = []
    for gi in groups:
        s = _mask_block_rows(s_cmp[gi], cmp_ok, tq)
        e = jnp.exp2(s - jnp.max(s, axis=0, keepdims=True))
        p = e * jnp.where(has_block, 1.0 / jnp.sum(e, axis=0, keepdims=True), 0.0)
        o_cmp.append(_dot(vct_ref[gi], p.astype(BF16)))
        psum = p[:, 0:tq]
        for h in range(1, H):
            psum = psum + p[:, h * tq:(h + 1) * tq]
        p_hi, p_lo = _split2(psum)
        imp = _dot(ovt, p_hi) + _dot(ovt, p_lo)
        imp = jnp.where(causal_blk, imp + jnp.where(forced, FORCE_BONUS, 0.0), MASK_VALUE)
        imp_groups = [imp[v * sub:(v + 1) * sub, :] for v in range(n_blk // sub)]
        rank = [jnp.zeros((sub, tq), F32) for _ in imp_groups]
        for j in range(n_blk):
            rj = imp[j:j + 1, :]
            for v, blk_imp in enumerate(imp_groups):
                if v * sub > j:
                    ahead = jnp.where(rj >= blk_imp, 1.0, 0.0)
                elif (v + 1) * sub <= j:
                    ahead = jnp.where(rj > blk_imp, 1.0, 0.0)
                else:
                    ahead = jnp.where(row_in_group > j - v * sub,
                                      jnp.where(rj >= blk_imp, 1.0, 0.0), jnp.where(rj > blk_imp, 1.0, 0.0))
                rank[v] = rank[v] + ahead
        rank = jnp.concatenate(rank, axis=0)
        thr_ref[gi] = jnp.where(rank < sel_k, (tlane - blk * SEL_LEN).astype(F32), -1.0)

    blocks_per_tile = tk // SEL_LEN
    n_kt = (t0 + tq + tk - 1) // tk
    last = n_kt - 1
    r_in_blk = lax.broadcasted_iota(jnp.int32, (SEL_LEN, tq), 0).astype(F32)

    def weighted_values(gi, kt, e_buf):
        return _dot(vst_ref[gi, :, pl.ds(pl.multiple_of(kt * tk, tk), tk)], e_buf[gi])

    def softmax_step(gi, kt, valid, s_buf, e_buf, m):
        rows = []
        for j in range(blocks_per_tile):
            limit = thr_ref[gi, pl.ds(kt * blocks_per_tile + j, 1), :]
            if valid is not None:
                limit = jnp.where(valid, limit, -1.0)
            rows.append(_mask_block_rows(s_buf[gi, j * SEL_LEN:(j + 1) * SEL_LEN, :], r_in_blk <= limit, tq))
        s = jnp.concatenate(rows, axis=0)
        m_new = jnp.maximum(m, jnp.max(s, axis=0, keepdims=True))
        e_buf[gi] = jnp.exp2((s - m_new).astype(BF16))
        return m_new, jnp.exp2(m - m_new)

    eb_ref[...] = jnp.zeros_like(eb_ref)

    def body(i, carry):
        kt_a = 2 * i
        kt_b = kt_a + 1
        ms, accs, alphas = carry
        pv_b = [weighted_values(gi, jnp.maximum(kt_a - 1, 0), eb_ref) for gi in groups]
        s_a_next = []
        for gi in groups:
            sb_ref[gi] = scores(gi, jnp.minimum(kt_b, last))
            s_a_next.append(scores(gi, jnp.minimum(kt_a + 2, last)))
        ms, alpha_a = zip(*[softmax_step(gi, kt_a, None, sa_ref, ea_ref, ms[gi]) for gi in groups])
        for gi in groups:
            sa_ref[gi] = s_a_next[gi]
        accs = [alphas[gi] * accs[gi] + pv_b[gi] for gi in groups]
        pv_a = [weighted_values(gi, kt_a, ea_ref) for gi in groups]
        ms, alphas = zip(*[softmax_step(gi, jnp.minimum(kt_b, last), kt_b <= last, sb_ref, eb_ref, ms[gi])
                           for gi in groups])
        accs = [alpha_a[gi] * accs[gi] + pv_a[gi] for gi in groups]
        return tuple(ms), tuple(accs), tuple(alphas)

    n_pairs = (n_kt + 1) // 2
    init = (tuple(jnp.full((1, nl), MASK_VALUE, F32) for _ in groups),
            tuple(jnp.zeros((D + V_AUG_ROWS, nl), F32) for _ in groups),
            tuple(jnp.zeros((1, nl), F32) for _ in groups))
    _, accs, alphas = lax.fori_loop(0, n_pairs, body, init)
    o_slc = []
    for gi in groups:
        acc = alphas[gi] * accs[gi] + weighted_values(gi, jnp.minimum(2 * n_pairs - 1, last), eb_ref)
        o_slc.append(acc[0:D] * (1.0 / acc[D:D + 1]))

    rel = tq_pos - wstart
    wrow = lax.broadcasted_iota(jnp.int32, (wlen, tq), 0)
    head_rows = (wrow[0:tq] <= rel) & (wrow[0:tq] > rel - WINDOW)
    tail_rows = wrow[tq:wlen] <= rel
    o_win = []
    for gi in groups:
        s = jnp.concatenate([_mask_block_rows(sw_ref[gi, 0:tq, :], head_rows, tq),
                             _mask_block_rows(sw_ref[gi, tq:wlen, :], tail_rows, tq)], axis=0)
        e = jnp.exp2((s - jnp.max(s, axis=0, keepdims=True)).astype(BF16))
        acc = _dot(vwt_ref[gi, :, pl.ds(wstart, wlen)], e)
        o_win.append(acc[0:D] * (1.0 / acc[D:D + 1]))

    gt_ref[...] = jnp.transpose(_sigmoid(gate_ref[...]))
    zb = zb_ref[...]
    outs = []
    for gi in groups:
        grp = pl.program_id(1) * GP + gi

        def gate_row(branch):
            base = (GATE_OFF - LANES) + branch * NSA_HEADS + grp * H
            return jnp.concatenate([gt_ref[pl.ds(base + h, 1), :] for h in range(H)], axis=1)

        mix = gate_row(0) * o_cmp[gi] + gate_row(1) * o_slc[gi] + gate_row(2) * o_win[gi]
        outs.append(jnp.transpose(jnp.concatenate([mix[:, h * tq:(h + 1) * tq] for h in range(H)], axis=0)))
    nat = jnp.concatenate(outs, axis=1) if GP > 1 else outs[0]
    o_ref[...] = (nat * (zb * _sigmoid(zb))).astype(BF16)


def _nsa_attn(q_heads, kc, vct, kv, proj, batch, seq, ovt, n_cmp, sel_k, tq=Q_TILE, gp=ATTN_GROUPS):
    t = batch * seq
    n_q = seq // tq
    assert tq == LANES, "the gate block is transposed as one 128 x 128 tile"
    tk = KEY_TILE if seq >= 2 * KEY_TILE else seq // 2
    wlen = WINDOW + tq if seq >= WINDOW + tq else seq
    ks, vst, kw, vwt = kv
    g, h, d = NSA_GROUPS, HEADS_PER_GROUP, HEAD_DIM
    row = lambda b, gi, i: b * n_q + i
    whole = lambda a: pl.BlockSpec((None, gp) + a.shape[2:], lambda b, gi, i: (b, gi, 0, 0))
    kc = kc.reshape((batch, g) + kc.shape[1:])
    vct = vct.reshape((batch, g) + vct.shape[1:])
    wide = gp * MXU_DIM
    return pl.pallas_call(
        functools.partial(_nsa_attn_kernel, n_cmp, sel_k),
        grid=(batch, g // gp, n_q),
        in_specs=[
            pl.BlockSpec((None, gp, h, tq, d), lambda b, gi, i: (b, gi, 0, i, 0)),
            whole(kc), whole(vct), whole(ks), whole(vst), whole(kw), whole(vwt),
            pl.BlockSpec((tq, LANES), lambda b, gi, i: (row(b, gi, i), (COL_MISC + LANES) // LANES)),
            pl.BlockSpec((tq, wide), lambda b, gi, i: (row(b, gi, i), COL_ZB // wide + gi)),
            pl.BlockSpec(ovt.shape, lambda b, gi, i: (0, 0)),
        ],
        out_specs=pl.BlockSpec((tq, wide), lambda b, gi, i: (row(b, gi, i), gi)),
        out_shape=jax.ShapeDtypeStruct((t, NSA_WIDTH), BF16),
        scratch_shapes=[pltpu.VMEM((gp, ovt.shape[0], tq), F32), pltpu.VMEM((LANES, tq), F32),
                        pltpu.VMEM((gp, tk, h * tq), F32), pltpu.VMEM((gp, tk, h * tq), F32),
                        pltpu.VMEM((gp, tk, h * tq), BF16), pltpu.VMEM((gp, tk, h * tq), BF16),
                        pltpu.VMEM((gp, wlen, h * tq), F32)],
        compiler_params=_params(("parallel", "parallel", "arbitrary")),
    )(q_heads, kc, vct, ks, vst, kw, vwt, proj, proj, ovt)


def _merge_kernel(ya_ref, yb_ref, wa_ref, wb_ref, ga_ref, gb_ref, o_ref):
    pa = _dot(ya_ref[...], wa_ref[...])
    pb = _dot(yb_ref[...], wb_ref[...])
    o_ref[...] = (_sigmoid(ga_ref[...]) * pa + _sigmoid(gb_ref[...]) * pb).astype(BF16)


def _merge(y_a, y_b, w_a, w_b, proj, layer, tm=512):
    t, c = y_a.shape
    yspec = pl.BlockSpec((tm, c), lambda i: (i, 0))
    wspec = pl.BlockSpec((None, c, D_MODEL), lambda i: (layer, 0, 0))
    return pl.pallas_call(
        _merge_kernel,
        grid=(t // tm,),
        in_specs=[yspec, yspec, wspec, wspec,
                  pl.BlockSpec((tm, D_MODEL), lambda i: (i, COL_MERGE // D_MODEL)),
                  pl.BlockSpec((tm, D_MODEL), lambda i: (i, COL_MERGE // D_MODEL + 1))],
        out_specs=pl.BlockSpec((tm, D_MODEL), lambda i: (i, 0)),
        out_shape=jax.ShapeDtypeStruct((t, D_MODEL), BF16),
        compiler_params=_params(("parallel",)),
    )(y_a, y_b, w_a, w_b, proj, proj)


def _outproj_kernel(m_ref, w_ref, g_ref, x_ref, o_ref):
    out = _dot(m_ref[...], w_ref[...])
    ms = jnp.mean(out * out, axis=-1, keepdims=True)
    o_ref[...] = x_ref[...] + out * lax.rsqrt(ms + NORM_EPS) * g_ref[...]


def _outproj(merged, w_out, gain, x, layer, tm=256):
    t = x.shape[0]
    rows = pl.BlockSpec((tm, D_MODEL), lambda i: (i, 0))
    return pl.pallas_call(
        _outproj_kernel,
        grid=(t // tm,),
        in_specs=[rows, pl.BlockSpec((None, D_MODEL, D_MODEL), lambda i: (layer, 0, 0)),
                  pl.BlockSpec((1, D_MODEL), lambda i: (0, 0)), rows],
        out_specs=rows,
        out_shape=jax.ShapeDtypeStruct((t, D_MODEL), F32),
        compiler_params=_params(("parallel",)),
    )(merged, w_out, gain, x)


def _reorder_w_in(w):
    u_end = SHIFT_WIDTH
    za = u_end
    q = za + RWKV_WIDTH
    kv = q + NSA_WIDTH
    gate = kv + 6 * NSA_KV_WIDTH
    zb = gate + 3 * NSA_HEADS
    merge = zb + NSA_WIDTH
    end = merge + 2 * D_MODEL
    lead = w.shape[:-1]
    pieces = [w[..., merge:end], w[..., 0:3 * RWKV_WIDTH], w[..., za:q], w[..., q:kv], w[..., kv:gate],
              w[..., zb:merge], w[..., 3 * RWKV_WIDTH:u_end], w[..., gate:zb],
              jnp.zeros(lead + (MISC_WIDTH - GATE_OFF - 3 * NSA_HEADS,), w.dtype),
              jnp.zeros(lead + (PROJ_WIDTH - COL_MISC - MISC_WIDTH,), w.dtype)]
    return jnp.concatenate([p.astype(BF16) for p in pieces], axis=-1)


def _rope_tables(pos):
    half = ROT_DIM // 2
    inv_freq = ROPE_THETA ** (-jnp.arange(half, dtype=F32) * (2.0 / ROT_DIM))
    ang = pos.astype(F32)[..., None] * inv_freq
    cos, sin = jnp.cos(ang), jnp.sin(ang)
    d = np.arange(LANES) % HEAD_DIM
    idx = d % half
    cos_l, sin_l = cos[..., idx], sin[..., idx]
    cos_t = jnp.where(d < ROT_DIM, cos_l, 1.0)
    sin_up = jnp.where((d >= half) & (d < ROT_DIM), sin_l, 0.0)
    sin_dn = jnp.where(d < half, -sin_l, 0.0)
    return cos_t, sin_up, sin_dn


def _constants(seq):
    n_cmp = (seq - CMP_LEN) // CMP_STRIDE + 1
    n_sel = seq // SEL_LEN
    n_pad = seq // CMP_STRIDE
    ones_bd = (np.arange(MXU_DIM)[:, None] // HEAD_DIM == np.arange(MXU_DIM)[None, :] // HEAD_DIM)
    sel = np.stack([np.arange(MXU_DIM)[:, None] == i * HEAD_DIM + np.arange(HEAD_DIM)[None, :]
                    for i in range(NSA_GROUPS)])
    ci = np.arange(n_pad)[:, None] * CMP_STRIDE
    sj = np.arange(n_sel)[None, :] * SEL_LEN
    overlap = (ci < sj + SEL_LEN) & (ci + CMP_LEN > sj) & (np.arange(n_pad)[:, None] < n_cmp)
    to_bf = lambda a: jnp.asarray(a.astype(np.float32), BF16)
    return dict(ones_bd=to_bf(ones_bd), sel=to_bf(sel), sel_t=to_bf(sel.transpose(0, 2, 1)),
                overlap_t=to_bf(overlap.T)), n_cmp, n_sel


def _pad_rows(w, rows, at):
    out = jnp.zeros((rows, w.shape[1]), w.dtype)
    return out.at[at:at + w.shape[0]].set(w)


def kernel(x, positions, norm_pre, norm_post, w_in, rwkv_mu, rwkv_w0, rwkv_w_up, rwkv_a0, rwkv_a_up,
           rwkv_k_k, rwkv_k_a, rwkv_r_k, rwkv_gn_w, rwkv_gn_b, rwkv_v0, rwkv_v_down, rwkv_v_up,
           nsa_pe_k, nsa_pe_v, nsa_ck_w1, nsa_ck_w2, nsa_cv_w1, nsa_cv_w2, w_proj_a, w_proj_b, w_out):
    batch, seq, d_model = x.shape
    depth = w_in.shape[0]
    t = batch * seq
    c = RWKV_WIDTH
    consts, n_cmp, n_sel = _constants(seq)
    sel_k = min(SEL_TOPK, n_sel)
    n_pad = seq // CMP_STRIDE

    tables = _rope_tables(positions.reshape(t))
    cmp_pos = positions[:, CMP_LEN - 1::CMP_STRIDE]
    cmp_pos = jnp.concatenate([cmp_pos, cmp_pos[:, -1:]], axis=1)
    cmp_tables = _rope_tables(cmp_pos)

    row = lambda a: a.reshape(1, -1).astype(F32)
    xf = x.reshape(t, d_model)
    v_first = None
    w_in_b = _reorder_w_in(w_in)
    w_a_b, w_b_b, w_out_b = w_proj_a.astype(BF16), w_proj_b.astype(BF16), w_out.astype(BF16)
    for l in range(depth):
        proj = _inproj(xf, row(norm_pre[l]), w_in_b, l)

        mu = rwkv_mu[l]
        rp = dict(
            mu_r=row(mu[0:c]), mu_k=row(mu[c:2 * c]), mu_v=row(mu[2 * c:3 * c]),
            mu_m=row(jnp.concatenate([mu[3 * c:], jnp.zeros((MISC_WIDTH - GATE_OFF,), F32)])),
            w0=row(rwkv_w0[l]), a0=row(rwkv_a0[l]), k_k=row(rwkv_k_k[l]), k_a=row(rwkv_k_a[l]),
            r_k=row(rwkv_r_k[l]),
            w_up=_pad_rows(rwkv_w_up[l], MISC_WIDTH, 0).astype(BF16),
            a_up=_pad_rows(rwkv_a_up[l], MISC_WIDTH, DECAY_LORA).astype(BF16),
            ones_bd=consts['ones_bd'], gn_w=row(rwkv_gn_w[l]), gn_b=row(rwkv_gn_b[l]))
        vres = None
        if l > 0:
            vres = dict(v0=row(rwkv_v0[l - 1]),
                        v_dn=jnp.pad(rwkv_v_down[l - 1], ((0, 0), (0, LANES - VRES_LORA))).astype(BF16),
                        v_up=_pad_rows(rwkv_v_up[l - 1], LANES, 0).astype(BF16))
        y_a, vf = _wkv(proj, batch, seq, rp, vres, v_first)
        if l == 0:
            v_first = vf

        q_heads, ks, vst, kw, vwt, kcc, vcc = _nsa_prep(proj, batch, seq, tables, consts['sel'],
                                                        consts['sel_t'])
        cmp_rows = lambda a: a.reshape(batch * NSA_GROUPS, n_pad, CMP_STRIDE * HEAD_DIM)
        cp = dict(pe_k=nsa_pe_k[l].reshape(1, -1), pe_v=nsa_pe_v[l].reshape(1, -1),
                  ck_w1=nsa_ck_w1[l].astype(BF16),
                  ck_w2=jnp.tile(nsa_ck_w2[l], (1, LANES // HEAD_DIM)).astype(BF16),
                  cv_w1=nsa_cv_w1[l].astype(BF16), cv_w2t=nsa_cv_w2[l].T.astype(BF16))
        kc, vct = _compress(cmp_rows(kcc), cmp_rows(vcc), cp, cmp_tables, n_cmp)
        y_b = _nsa_attn(q_heads, kc, vct, (ks, vst, kw, vwt), proj, batch, seq, consts['overlap_t'],
                        n_cmp, sel_k)

        merged = _merge(y_a, y_b, w_a_b, w_b_b, proj, l)
        xf = _outproj(merged, w_out_b, row(norm_post[l]), xf, l)
    return xf.reshape(batch, seq, d_model)
```
